```python
import math
import jax
import jax.numpy as jnp
from jax import lax
import numpy as np

D_MODEL = 2048
BATCH = 16
SEQ = 2048
DEPTH = 2
DEC_BATCH = 4
DEC_SEQ = 2048
PAST_LEN = 128

A_WIDTH = D_MODEL // 4
B_WIDTH = D_MODEL // 2
C_WIDTH = D_MODEL // 4
A_DK = 128
A_DV = 128
A_HEADS = A_WIDTH // A_DV
A_QK = A_HEADS * A_DK
HGRN_CHUNK = 64
LB_MIN = 1e-12
B_DV = 128
B_HEADS = B_WIDTH // B_DV
B_DQK = B_DV // 2
ROT_DIM = B_DQK // 4
ROPE_THETA = 500000.0
Q_BLOCK = 128
C_BLOCKS = 8
C_BLOCK = C_WIDTH // C_BLOCKS
CONV_W = 4
CONV_LEFT = 2
RG_C = 8.0
N_EXPERTS = 16
EC_FACTOR = 2
D_EXPERT = D_MODEL // 2
IN_SIZES = (A_QK, A_QK, A_QK, A_WIDTH, A_WIDTH, B_WIDTH, B_WIDTH, B_WIDTH, C_WIDTH, C_WIDTH)
IN_COLS = 3 * A_QK + 2 * A_WIDTH + 3 * B_WIDTH + 2 * C_WIDTH
N_MOD = 6

kernel_name = "hybrid_hgrn2_diffattn_rglru_ec_encoder"


def rms_norm(x, w, eps=1e-6):
    xf = x.astype(jnp.float32)
    y = xf * lax.rsqrt(jnp.mean(xf * xf, axis=-1, keepdims=True) + eps)
    return (y * w.astype(jnp.float32)).astype(x.dtype)


def partial_rope(x):
    L = x.shape[1]
    half = ROT_DIM // 2
    inv_freq = ROPE_THETA ** (-jnp.arange(half, dtype=jnp.float32) / half)
    ang = jnp.arange(L, dtype=jnp.float32)[:, None] * inv_freq[None, :]
    cos = jnp.cos(ang)[None, :, None, None, :]
    sin = jnp.sin(ang)[None, :, None, None, :]
    xr = x[..., :ROT_DIM].astype(jnp.float32)
    x1, x2 = xr[..., :half], xr[..., half:]
    rot = jnp.concatenate([x1 * cos - x2 * sin, x2 * cos + x1 * sin], axis=-1)
    return jnp.concatenate([rot.astype(x.dtype), x[..., ROT_DIM:]], axis=-1)


def hgrn2_scan(q, k, log_f, v):
    Bn, L, H, dk = q.shape
    dv = v.shape[-1]
    n = L // HGRN_CHUNK

    def chunks(t):
        return t.reshape(Bn, n, HGRN_CHUNK, H, t.shape[-1]).transpose(1, 0, 3, 2, 4)

    lower_tri = jnp.tril(jnp.ones((HGRN_CHUNK, HGRN_CHUNK), dtype=bool))[:, :, None]

    def step(state, inp):
        qc, kc, gc, vc = inp
        b = jnp.cumsum(gc, axis=2)
        rel = b[:, :, :, None, :] - b[:, :, None, :, :]
        decay = jnp.where(lower_tri, jnp.exp(jnp.minimum(rel, 0.0)), 0.0)
        scores = jnp.einsum('bhtk,bhsk,bhtsk->bhts', qc, kc, decay)
        out = (jnp.einsum('bhts,bhsv->bhtv', scores, vc)
               + jnp.einsum('bhtk,bhkv->bhtv', qc * jnp.exp(b), state))
        b_end = b[:, :, -1:, :]
        state = (state * jnp.exp(b_end)[:, :, 0, :, None]
                 + jnp.einsum('bhsk,bhsv->bhkv', kc * jnp.exp(b_end - b), vc))
        return state, out

    s0 = jnp.zeros((Bn, H, dk, dv), jnp.float32)
    _, out = lax.scan(step, s0, (chunks(q), chunks(k), chunks(log_f), chunks(v)))
    return out.transpose(1, 0, 3, 2, 4).reshape(Bn, L, H, dv)


def hgrn2_mixer(q_lin, f_fwd, f_bwd, i_in, g_in, lb, norm_w):
    Bn, L, _ = q_lin.shape

    def heads(t):
        return t.astype(jnp.float32).reshape(Bn, L, A_HEADS, -1)

    lb = lb.reshape(A_HEADS, A_DK)
    log_lb = jnp.log(jnp.maximum(lb, LB_MIN))
    log_ub = jnp.log1p(-lb)

    def gates(z):
        z = heads(z)
        log_f = jnp.logaddexp(log_lb, log_ub + jax.nn.log_sigmoid(z))
        k = (1.0 - lb) * jax.nn.sigmoid(-z)
        return log_f, k

    q = jax.nn.silu(heads(q_lin))
    v = heads(i_in)
    lf_f, k_f = gates(f_fwd)
    lf_b, k_b = gates(f_bwd)
    flip = lambda t: jnp.flip(t, axis=1)
    o = hgrn2_scan(q, k_f, lf_f, v) + flip(hgrn2_scan(flip(q), flip(k_b), flip(lf_b), flip(v)))
    o = rms_norm(o, norm_w) * jax.nn.silu(heads(g_in))
    return o.reshape(Bn, L, A_WIDTH).astype(q_lin.dtype)


def diff_attention(q, k, v, lam, lam_init, sub_w):
    Bn, L = q.shape[:2]
    nb = L // Q_BLOCK
    scale = B_DQK ** -0.5
    vf = v.astype(jnp.float32)
    qb = jnp.moveaxis(q.reshape(Bn, nb, Q_BLOCK, B_HEADS, 2, B_DQK), 1, 0)

    def block(qi):
        s = jnp.einsum('bqhcd,bkhcd->bhcqk', qi, k, preferred_element_type=jnp.float32) * scale
        p = jax.nn.softmax(s, axis=-1)
        w = p[:, :, 0] - lam * p[:, :, 1]
        return jnp.einsum('bhqk,bkhe->bqhe', w, vf)

    o = lax.map(block, qb)
    o = jnp.moveaxis(o, 0, 1).reshape(Bn, L, B_HEADS, B_DV)
    o = rms_norm(o, sub_w, eps=1e-5) * (1.0 - lam_init)
    return o.reshape(Bn, L, B_WIDTH).astype(q.dtype)


def _linear_combine(e1, e2):
    a1, b1 = e1
    a2, b2 = e2
    return a1 * a2, a2 * b1 + b2


def rglru_mixer(xb, gb, conv_w, conv_b, wa, ba, wx, bx, lam, norm_w):
    Bn, L, W = xb.shape
    xp = jnp.pad(xb, ((0, 0), (CONV_LEFT, CONV_W - 1 - CONV_LEFT), (0, 0)))
    u = conv_b
    for j in range(CONV_W):
        u = u + xp[:, j:j + L, :] * conv_w[j]
    u = u.astype(jnp.float32)
    ub = u.reshape(Bn, L, C_BLOCKS, C_BLOCK)
    h_dirs = []
    for d in range(2):
        r = jax.nn.sigmoid(jnp.einsum('blnc,nce->blne', ub, wa[d]).reshape(Bn, L, W) + ba[d])
        ig = jax.nn.sigmoid(jnp.einsum('blnc,nce->blne', ub, wx[d]).reshape(Bn, L, W) + bx[d])
        log_a = -RG_C * r * jax.nn.softplus(-lam[d])
        a = jnp.exp(log_a)
        inp = jnp.sqrt(-jnp.expm1(2.0 * log_a)) * (ig * u)
        if d == 1:
            a, inp = jnp.flip(a, axis=1), jnp.flip(inp, axis=1)
        h = lax.associative_scan(_linear_combine, (a, inp), axis=1)[1]
        if d == 1:
            h = jnp.flip(h, axis=1)
        h_dirs.append(h)
    y = (h_dirs[0] + h_dirs[1]) * jax.nn.gelu(gb.astype(jnp.float32))
    return rms_norm(y, norm_w).astype(xb.dtype)


def expert_choice_ffn(x, w_router, w_gate, w_up, w_down):
    N, D = x.shape
    cap = max(1, EC_FACTOR * N // N_EXPERTS)
    probs = jax.nn.softmax(jnp.matmul(x, w_router).astype(jnp.float32), axis=-1)
    gates, idx = lax.top_k(probs.T, cap)
    xe = x[idx]
    h = jax.nn.silu(jnp.einsum('ecd,edf->ecf', xe, w_gate)) * jnp.einsum('ecd,edf->ecf', xe, w_up)
    ye = jnp.einsum('ecf,efd->ecd', h, w_down) * gates[..., None].astype(x.dtype)
    return jnp.zeros_like(x).at[idx.reshape(-1)].add(ye.reshape(-1, D).astype(x.dtype))


def trunk(x, c, ada_w, ada_b, norm_mix_pre, norm_mix_post, norm_ffn_pre, norm_ffn_post,
          w_in, hg_lower, hg_norm, dl_q1, dl_k1, dl_q2, dl_k2, dl_subln,
          conv_w, conv_b, rg_wa, rg_ba, rg_wx, rg_bx, rg_lambda, rg_norm, w_out,
          w_router, w_gate, w_up, w_down):
    Bn, L, D = x.shape
    bounds = np.cumsum(IN_SIZES)[:-1].tolist()
    lb_soft = jax.nn.softmax(hg_lower.astype(jnp.float32), axis=0)
    lb_all = jnp.cumsum(lb_soft, axis=0) - lb_soft[0:1]
    for l in range(DEPTH):
        mod = jnp.matmul(jax.nn.silu(c), ada_w[l]) + ada_b[l]
        sh_m, sc_m, g_m, sh_f, sc_f, g_f = jnp.split(mod[:, None, :], N_MOD, axis=-1)

        h = rms_norm(x, norm_mix_pre[l]) * (1.0 + sc_m) + sh_m
        proj = jnp.matmul(h, w_in[l])
        a_q, a_ff, a_fb, a_i, a_g, b_q, b_k, b_v, c_x, c_g = jnp.split(proj, bounds, axis=-1)

        o_a = hgrn2_mixer(a_q, a_ff, a_fb, a_i, a_g, lb_all[l], hg_norm[l])

        lam_init = 0.8 - 0.6 * math.exp(-0.3 * l)
        lam = (jnp.exp(jnp.sum(dl_q1[l].astype(jnp.float32) * dl_k1[l].astype(jnp.float32)))
               - jnp.exp(jnp.sum(dl_q2[l].astype(jnp.float32) * dl_k2[l].astype(jnp.float32)))
               + lam_init)
        qd = partial_rope(b_q.reshape(Bn, L, B_HEADS, 2, B_DQK))
        kd = partial_rope(b_k.reshape(Bn, L, B_HEADS, 2, B_DQK))
        vd = b_v.reshape(Bn, L, B_HEADS, B_DV)
        o_b = diff_attention(qd, kd, vd, lam, lam_init, dl_subln[l])

        o_c = rglru_mixer(c_x, c_g, conv_w[l], conv_b[l], rg_wa[l], rg_ba[l],
                          rg_wx[l], rg_bx[l], rg_lambda[l], rg_norm[l])

        mix = jnp.matmul(jnp.concatenate([o_a, o_b, o_c], axis=-1), w_out[l])
        x = x + g_m * rms_norm(mix, norm_mix_post[l])

        h = rms_norm(x, norm_ffn_pre[l]) * (1.0 + sc_f) + sh_f
        y = expert_choice_ffn(h.reshape(Bn * L, D), w_router[l], w_gate[l], w_up[l], w_down[l])
        x = x + g_f * rms_norm(y.reshape(Bn, L, D), norm_ffn_post[l])
    return x


def setup_inputs(seed: int = 0) -> dict:
    key = jax.random.key(seed)
    keys = jax.random.split(key, 40)
    cnt = [0]

    def nxt():
        k = keys[cnt[0]]
        cnt[0] += 1
        return k

    def nrm(shape, scale):
        return jax.random.normal(nxt(), shape, jnp.float32) * scale

    def gain(shape):
        return 1.0 + nrm(shape, 0.02)

    D = D_MODEL
    x_prompt = nrm((BATCH, SEQ, D), 1.0)
    x_sample = nrm((DEC_BATCH, DEC_SEQ, D), 1.0)
    c_prompt = nrm((BATCH, D), 1.0)
    c_sample = nrm((DEC_BATCH, D), 1.0)
    ada_w = nrm((DEPTH, D, N_MOD * D), 0.5 * D ** -0.5)
    ada_b = nrm((DEPTH, N_MOD * D), 0.02)
    norm_mix_pre = gain((DEPTH, D))
    norm_mix_post = gain((DEPTH, D))
    norm_ffn_pre = gain((DEPTH, D))
    norm_ffn_post = gain((DEPTH, D))
    w_in = nrm((DEPTH, D, IN_COLS), D ** -0.5)
    hg_lower = nrm((DEPTH, A_QK), 1.0)
    hg_norm = gain((DEPTH, A_DV))
    dl_q1 = nrm((DEPTH, B_DQK), 0.1)
    dl_k1 = nrm((DEPTH, B_DQK), 0.1)
    dl_q2 = nrm((DEPTH, B_DQK), 0.1)
    dl_k2 = nrm((DEPTH, B_DQK), 0.1)
    dl_subln = gain((DEPTH, B_DV))
    conv_w = nrm((DEPTH, CONV_W, C_WIDTH), CONV_W ** -0.5)
    conv_b = nrm((DEPTH, C_WIDTH), 0.02)
    rg_wa = nrm((DEPTH, 2, C_BLOCKS, C_BLOCK, C_BLOCK), C_BLOCK ** -0.5)
    rg_ba = nrm((DEPTH, 2, C_WIDTH), 0.02)
    rg_wx = nrm((DEPTH, 2, C_BLOCKS, C_BLOCK, C_BLOCK), C_BLOCK ** -0.5)
    rg_bx = nrm((DEPTH, 2, C_WIDTH), 0.02)
    a_c = jax.random.uniform(nxt(), (DEPTH, 2, C_WIDTH), jnp.float32, 0.9, 0.999)
    a0 = a_c ** (1.0 / RG_C)
    rg_lambda = jnp.log(a0) - jnp.log1p(-a0)
    rg_norm = gain((DEPTH, C_WIDTH))
    w_out = nrm((DEPTH, D, D), D ** -0.5)
    w_router = nrm((DEPTH, D, N_EXPERTS), D ** -0.5)
    w_gate = nrm((DEPTH, N_EXPERTS, D, D_EXPERT), D ** -0.5)
    w_up = nrm((DEPTH, N_EXPERTS, D, D_EXPERT), D ** -0.5)
    w_down = nrm((DEPTH, N_EXPERTS, D_EXPERT, D), D_EXPERT ** -0.5)
    return {
        'x_prompt': x_prompt, 'x_sample': x_sample, 'c_prompt': c_prompt, 'c_sample': c_sample,
        'ada_w': ada_w, 'ada_b': ada_b,
        'norm_mix_pre': norm_mix_pre, 'norm_mix_post': norm_mix_post,
        'norm_ffn_pre': norm_ffn_pre, 'norm_ffn_post': norm_ffn_post,
        'w_in': w_in, 'hg_lower': hg_lower, 'hg_norm': hg_norm,
        'dl_q1': dl_q1, 'dl_k1': dl_k1, 'dl_q2': dl_q2, 'dl_k2': dl_k2, 'dl_subln': dl_subln,
        'conv_w': conv_w, 'conv_b': conv_b, 'rg_wa': rg_wa, 'rg_ba': rg_ba,
        'rg_wx': rg_wx, 'rg_bx': rg_bx, 'rg_lambda': rg_lambda, 'rg_norm': rg_norm,
        'w_out': w_out, 'w_router': w_router, 'w_gate': w_gate, 'w_up': w_up, 'w_down': w_down,
    }


def reference(x_prompt, x_sample, c_prompt, c_sample, ada_w, ada_b,
              norm_mix_pre, norm_mix_post, norm_ffn_pre, norm_ffn_post,
              w_in, hg_lower, hg_norm, dl_q1, dl_k1, dl_q2, dl_k2, dl_subln,
              conv_w, conv_b, rg_wa, rg_ba, rg_wx, rg_bx, rg_lambda, rg_norm,
              w_out, w_router, w_gate, w_up, w_down):
    y_prompt = trunk(x_prompt, c_prompt, ada_w, ada_b, norm_mix_pre, norm_mix_post,
                     norm_ffn_pre, norm_ffn_post, w_in, hg_lower, hg_norm,
                     dl_q1, dl_k1, dl_q2, dl_k2, dl_subln, conv_w, conv_b,
                     rg_wa, rg_ba, rg_wx, rg_bx, rg_lambda, rg_norm, w_out,
                     w_router, w_gate, w_up, w_down)
    y_sample = trunk(x_sample, c_sample, ada_w, ada_b, norm_mix_pre, norm_mix_post,
                     norm_ffn_pre, norm_ffn_post, w_in, hg_lower, hg_norm,
                     dl_q1, dl_k1, dl_q2, dl_k2, dl_subln, conv_w, conv_b,
                     rg_wa, rg_ba, rg_wx, rg_bx, rg_lambda, rg_norm, w_out,
                     w_router, w_gate, w_up, w_down)
    return (y_prompt, y_sample)
```

```python
import functools
import math

import jax
import jax.numpy as jnp
from jax import lax
from jax.experimental import pallas as pl
from jax.experimental.pallas import tpu as pltpu

F32 = jnp.float32
BF16 = jnp.bfloat16
I32 = jnp.int32
HIGHEST = lax.Precision.HIGHEST

D_MODEL = 2048
DEPTH = 2
A_WIDTH = D_MODEL // 4
B_WIDTH = D_MODEL // 2
C_WIDTH = D_MODEL // 4
HEAD = 128
A_HEADS = A_WIDTH // HEAD
B_HEADS = B_WIDTH // HEAD
B_DQK = HEAD // 2
ROT_DIM = B_DQK // 4
ROPE_THETA = 500000.0
LB_MIN = 1e-12
C_BLOCKS = 8
C_BLOCK = C_WIDTH // C_BLOCKS
RG_C = 8.0
N_EXPERTS = 16
EC_FACTOR = 2
D_EXPERT = D_MODEL // 2
N_MOD = 6
IN_COLS = 3 * A_WIDTH + 2 * A_WIDTH + 3 * B_WIDTH + 2 * C_WIDTH
COL_AQ, COL_AFF, COL_AFB, COL_AI, COL_AG = 0, 512, 1024, 1536, 2048
COL_BQ, COL_BK, COL_BV = 2560, 3584, 4608
COL_CX, COL_CG = 5632, 6144

LANES = 128
SUBLANES = 8
VMEM_LIMIT = 56 * 1024 * 1024

HGRN_CHUNK = 64
HGRN_SUB = 16


def _cparams(sem):
    return pltpu.CompilerParams(dimension_semantics=sem, vmem_limit_bytes=VMEM_LIMIT)


def _sigmoid(x):
    return 1.0 / (1.0 + jnp.exp(-x))


def _softplus(y):
    return jnp.maximum(y, 0.0) + jnp.log1p(jnp.exp(-jnp.abs(y)))


def _logaddexp(a, c):
    return jnp.maximum(a, c) + jnp.log1p(jnp.exp(-jnp.abs(a - c)))


def _rms(x, eps):
    return x * lax.rsqrt(jnp.mean(x * x, axis=-1, keepdims=True) + eps)


def _dot_nt(a, b, **kw):
    return lax.dot_general(a, b, (((1,), (1,)), ((), ())), preferred_element_type=F32, **kw)


def _dot_tn(a, b):
    return lax.dot_general(a, b, (((0,), (0,)), ((), ())), preferred_element_type=F32)


def _ada_kernel(c_ref, w_ref, b_ref, o_ref):
    c = c_ref[...]
    a = (c * _sigmoid(c)).astype(BF16)
    o_ref[0] = jnp.dot(a, w_ref[0].astype(BF16), preferred_element_type=F32) + b_ref[0]


def _ada_call(c, ada_w, ada_b):
    bn = c.shape[0]
    tn = 1024
    ncol = N_MOD * D_MODEL
    return pl.pallas_call(
        _ada_kernel,
        grid=(DEPTH, ncol // tn),
        in_specs=[
            pl.BlockSpec((bn, D_MODEL), lambda l, j: (0, 0)),
            pl.BlockSpec((1, D_MODEL, tn), lambda l, j: (l, 0, j)),
            pl.BlockSpec((1, 1, tn), lambda l, j: (l, 0, j)),
        ],
        out_specs=pl.BlockSpec((1, bn, tn), lambda l, j: (l, 0, j)),
        out_shape=jax.ShapeDtypeStruct((DEPTH, bn, ncol), F32),
        compiler_params=_cparams(("arbitrary", "arbitrary")),
    )(c, ada_w, ada_b.reshape(DEPTH, 1, ncol))


def _in_kernel(x_ref, nw_ref, sc_ref, sh_ref, w_ref, o_ref, h_scr):
    @pl.when(pl.program_id(1) == 0)
    def _():
        h = _rms(x_ref[0], 1e-6) * nw_ref[...]
        h = h * (1.0 + sc_ref[0]) + sh_ref[0]
        h_scr[...] = h.astype(BF16)

    o_ref[0] = jnp.dot(h_scr[...], w_ref[...], preferred_element_type=F32)


def _in_call(x, nw, sc, sh, w_bf16):
    bn, L, _ = x.shape
    tm = min(1024, L)
    tn = 512
    lt = L // tm
    return pl.pallas_call(
        _in_kernel,
        grid=(bn * lt, IN_COLS // tn),
        in_specs=[
            pl.BlockSpec((1, tm, D_MODEL), lambda i, j: (i // lt, i % lt, 0)),
            pl.BlockSpec((1, D_MODEL), lambda i, j: (0, 0)),
            pl.BlockSpec((1, 1, D_MODEL), lambda i, j: (i // lt, 0, 0)),
            pl.BlockSpec((1, 1, D_MODEL), lambda i, j: (i // lt, 0, 0)),
            pl.BlockSpec((D_MODEL, tn), lambda i, j: (0, j)),
        ],
        out_specs=pl.BlockSpec((1, tm, tn), lambda i, j: (i // lt, i % lt, j)),
        out_shape=jax.ShapeDtypeStruct((bn, L, IN_COLS), F32),
        scratch_shapes=[pltpu.VMEM((tm, D_MODEL), BF16)],
        compiler_params=_cparams(("arbitrary", "arbitrary")),
    )(x, nw.reshape(1, D_MODEL), sc, sh, w_bf16)


def _hgrn_kernel(q_ref, ff_ref, fb_ref, i_ref, g_ref, par_ref, nw_ref, o_ref, acc_scr, st_scr, *, L):
    C, SB = HGRN_CHUNK, HGRN_SUB
    nsb = C // SB
    nchunk = L // C
    log_lb = par_ref[0, 0:1, :]
    log_ub = par_ref[0, 1:2, :]
    one_m_lb = par_ref[0, 2:3, :]
    row = lax.broadcasted_iota(I32, (C, C), 0)
    col = lax.broadcasted_iota(I32, (C, C), 1)
    sub_row = lax.broadcasted_iota(I32, (SB, 1), 0)

    def run_dir(f_ref, rev, first):
        tri = (col >= row).astype(F32) if rev else (col <= row).astype(F32)
        order = list(range(nsb - 1, -1, -1)) if rev else list(range(nsb))
        st_scr[...] = jnp.zeros_like(st_scr)

        def body(ci, carry):
            c = (nchunk - 1 - ci) if rev else ci
            r0 = pl.multiple_of(c * C, C)
            z = f_ref[0, pl.ds(r0, C), :]
            ql = q_ref[0, pl.ds(r0, C), :]
            v = i_ref[0, pl.ds(r0, C), :]
            q = ql * _sigmoid(ql)
            log_f = _logaddexp(log_lb, log_ub - _softplus(-z))
            k = one_m_lb * _sigmoid(-z)
            b = jnp.dot(tri, log_f, precision=HIGHEST, preferred_element_type=F32)
            st = st_scr[...]
            o_state = _dot_nt((q * jnp.exp(b)).astype(BF16), st.astype(BF16))
            v_bf = v.astype(BF16)
            for p, blk in enumerate(order):
                lo = SB * blk
                b_blk = b[lo:lo + SB]
                q_blk = q[lo:lo + SB]
                k_blk = k[lo:lo + SB]
                v_blk = v[lo:lo + SB]
                out = o_state[lo:lo + SB]
                if p > 0:
                    if rev:
                        bound = b[lo + SB:lo + SB + 1]
                        e0, e1 = lo + SB, C
                    else:
                        bound = b[lo - 1:lo]
                        e0, e1 = 0, lo
                    qt = (q_blk * jnp.exp(b_blk - bound)).astype(BF16)
                    kt = (k[e0:e1] * jnp.exp(bound - b[e0:e1])).astype(BF16)
                    sc = _dot_nt(qt, kt)
                    out = out + jnp.dot(sc.astype(BF16), v_bf[e0:e1], preferred_element_type=F32)
                diag = jnp.zeros((SB, HEAD), F32)
                for t in range(SB):
                    bt = b_blk[t:t + 1]
                    pm = k_blk * jnp.exp(jnp.minimum(bt - b_blk, 0.0)) * q_blk[t:t + 1]
                    s = jnp.sum(pm, axis=-1, keepdims=True)
                    keep = (sub_row >= t) if rev else (sub_row <= t)
                    s = jnp.where(keep, s, 0.0)
                    o_t = jnp.sum(s * v_blk, axis=0, keepdims=True)
                    diag = jnp.where(sub_row == t, o_t, diag)
                out = out + diag
                rows = pl.ds(r0 + lo, SB)
                if first:
                    acc_scr[rows, :] = out
                else:
                    acc_scr[rows, :] = acc_scr[rows, :] + out
            b_end = b[0:1] if rev else b[C - 1:C]
            kend = (k * jnp.exp(b_end - b)).astype(BF16)
            st_scr[...] = st * jnp.exp(b_end) + _dot_tn(v_bf, kend)
            return carry

        lax.fori_loop(0, nchunk, body, 0)

    run_dir(ff_ref, False, True)
    run_dir(fb_ref, True, False)
    g = g_ref[0]
    o_ref[0] = _rms(acc_scr[...], 1e-6) * nw_ref[...] * (g * _sigmoid(g))


def _hgrn_call(proj, par, nw):
    bn, L, _ = proj.shape
    cb = lambda off: (lambda b, h: (b, 0, off // HEAD + h))
    blk = (1, L, HEAD)
    return pl.pallas_call(
        functools.partial(_hgrn_kernel, L=L),
        grid=(bn, A_HEADS),
        in_specs=[
            pl.BlockSpec(blk, cb(COL_AQ)),
            pl.BlockSpec(blk, cb(COL_AFF)),
            pl.BlockSpec(blk, cb(COL_AFB)),
            pl.BlockSpec(blk, cb(COL_AI)),
            pl.BlockSpec(blk, cb(COL_AG)),
            pl.BlockSpec((1, 3, HEAD), lambda b, h: (h, 0, 0)),
            pl.BlockSpec((1, HEAD), lambda b, h: (0, 0)),
        ],
        out_specs=pl.BlockSpec(blk, lambda b, h: (b, 0, h)),
        out_shape=jax.ShapeDtypeStruct((bn, L, A_WIDTH), F32),
        scratch_shapes=[pltpu.VMEM((L, HEAD), F32), pltpu.VMEM((HEAD, HEAD), F32)],
        compiler_params=_cparams(("arbitrary", "arbitrary")),
    )(proj, proj, proj, proj, proj, par, nw.reshape(1, HEAD))


def _qkv_kernel(q_ref, k_ref, v_ref, cos_ref, s1_ref, s2_ref, qo_ref, ko_ref, vo_ref):
    cos = cos_ref[...]
    s1 = s1_ref[...]
    s2 = s2_ref[...]
    scale = B_DQK ** -0.5
    for h in range(q_ref.shape[2] // HEAD):
        sl = slice(HEAD * h, HEAD * (h + 1))
        for src, dst, mul in ((q_ref, qo_ref, scale), (k_ref, ko_ref, 1.0)):
            x = src[0, :, sl]
            xr = x * cos + pltpu.roll(x, HEAD - ROT_DIM // 2, 1) * s1 + pltpu.roll(x, ROT_DIM // 2, 1) * s2
            dst[0, :, sl] = (xr * mul).astype(BF16)
    vo_ref[0] = v_ref[0].astype(BF16)


def _qkv_call(proj, cos_t, s1_t, s2_t):
    bn, L, _ = proj.shape
    tl = min(512, L)
    wb = 512
    nj = B_WIDTH // wb
    cb = lambda off: (lambda b, i, j: (b, i, off // wb + j))
    out = jax.ShapeDtypeStruct((bn, L, B_WIDTH), BF16)
    ospec = pl.BlockSpec((1, tl, wb), lambda b, i, j: (b, i, j))
    tspec = pl.BlockSpec((tl, HEAD), lambda b, i, j: (i, 0))
    return pl.pallas_call(
        _qkv_kernel,
        grid=(bn, L // tl, nj),
        in_specs=[
            pl.BlockSpec((1, tl, wb), cb(COL_BQ)),
            pl.BlockSpec((1, tl, wb), cb(COL_BK)),
            pl.BlockSpec((1, tl, wb), cb(COL_BV)),
            tspec, tspec, tspec,
        ],
        out_specs=[ospec, ospec, ospec],
        out_shape=[out, out, out],
        compiler_params=_cparams(("arbitrary", "arbitrary", "arbitrary")),
    )(proj, proj, proj, cos_t, s1_t, s2_t)


def _attn_kernel(lam_ref, q_ref, k_ref, v_ref, sw_ref, o_ref):
    lam = lam_ref[0]
    post = lam_ref[1]
    q = q_ref[0]
    k = k_ref[0]
    lane = lax.broadcasted_iota(I32, (1, HEAD), 1)
    zero = jnp.zeros_like(q)
    q1 = jnp.where(lane < B_DQK, q, zero)
    q2 = jnp.where(lane >= B_DQK, q, zero)

    def softmax_parts(s):
        m = jnp.max(s, axis=-1, keepdims=True)
        p = jnp.exp(s - m)
        return p, jnp.sum(p, axis=-1, keepdims=True)

    p1, l1 = softmax_parts(_dot_nt(q1, k))
    p2, l2 = softmax_parts(_dot_nt(q2, k))
    w = p1 * (1.0 / l1) - p2 * (lam / l2)
    o = jnp.dot(w.astype(BF16), v_ref[0], preferred_element_type=F32)
    o_ref[0] = _rms(o, 1e-5) * sw_ref[...] * post


def _attn_call(lam2, qb, kb, vb, sw):
    bn, L, _ = qb.shape
    tq = min(256, L)
    return pl.pallas_call(
        _attn_kernel,
        grid=(bn, B_HEADS, L // tq),
        in_specs=[
            pl.BlockSpec(memory_space=pltpu.SMEM),
            pl.BlockSpec((1, tq, HEAD), lambda b, h, i: (b, i, h)),
            pl.BlockSpec((1, L, HEAD), lambda b, h, i: (b, 0, h)),
            pl.BlockSpec((1, L, HEAD), lambda b, h, i: (b, 0, h)),
            pl.BlockSpec((1, HEAD), lambda b, h, i: (0, 0)),
        ],
        out_specs=pl.BlockSpec((1, tq, HEAD), lambda b, h, i: (b, i, h)),
        out_shape=jax.ShapeDtypeStruct((bn, L, B_WIDTH), F32),
        compiler_params=_cparams(("arbitrary", "arbitrary", "arbitrary")),
    )(lam2, qb, kb, vb, sw.reshape(1, HEAD))


def _rglru_kernel(x_ref, g_ref, cw_ref, cb_ref, wg_ref, bg_ref, c8_ref, o_ref,
                  xs, a_f, x_f, a_b, x_b, h_f, *, L):
    pad = SUBLANES
    xs[0:pad, :] = jnp.zeros((pad, HEAD), F32)
    xs[pad + L:2 * pad + L, :] = jnp.zeros((pad, HEAD), F32)
    xs[pad:pad + L, :] = x_ref[0]
    cw = cw_ref[...]
    cb = cb_ref[...]
    wg = wg_ref[0]
    bg = bg_ref[0]
    c8 = c8_ref[0]
    tc = min(256, L)
    for ci in range(L // tc):
        r0 = ci * tc
        u = cb
        for j in range(4):
            u = u + xs[pad - 2 + j + r0:pad - 2 + j + r0 + tc, :] * cw[j:j + 1]
        gates = jnp.dot(u.astype(BF16), wg, preferred_element_type=F32)
        for d, (a_scr, x_scr) in enumerate(((a_f, x_f), (a_b, x_b))):
            r = _sigmoid(gates[:, (2 * d) * HEAD:(2 * d + 1) * HEAD] + bg[2 * d:2 * d + 1])
            ig = _sigmoid(gates[:, (2 * d + 1) * HEAD:(2 * d + 2) * HEAD] + bg[2 * d + 1:2 * d + 2])
            log_a = c8[d:d + 1] * r
            a = jnp.exp(log_a)
            a_scr[r0:r0 + tc, :] = a
            x_scr[r0:r0 + tc, :] = jnp.sqrt(1.0 - a * a) * (ig * u)

    rowi = lax.broadcasted_iota(I32, (SUBLANES, HEAD), 0)
    nblk = L // SUBLANES

    def fwd(i, h):
        r0 = pl.multiple_of(i * SUBLANES, SUBLANES)
        a = a_f[pl.ds(r0, SUBLANES), :]
        x = x_f[pl.ds(r0, SUBLANES), :]
        for s in (1, 2, 4):
            ok = rowi >= s
            a_s = jnp.where(ok, pltpu.roll(a, s, 0), 1.0)
            x_s = jnp.where(ok, pltpu.roll(x, s, 0), 0.0)
            x = a * x_s + x
            a = a * a_s
        hh = x + a * h
        h_f[pl.ds(r0, SUBLANES), :] = hh
        return hh[SUBLANES - 1:SUBLANES]

    lax.fori_loop(0, nblk, fwd, jnp.zeros((1, HEAD), F32))

    def bwd(i, h):
        r0 = pl.multiple_of((nblk - 1 - i) * SUBLANES, SUBLANES)
        a = a_b[pl.ds(r0, SUBLANES), :]
        x = x_b[pl.ds(r0, SUBLANES), :]
        for s in (1, 2, 4):
            ok = rowi < SUBLANES - s
            a_s = jnp.where(ok, pltpu.roll(a, SUBLANES - s, 0), 1.0)
            x_s = jnp.where(ok, pltpu.roll(x, SUBLANES - s, 0), 0.0)
            x = a * x_s + x
            a = a * a_s
        hh = x + a * h
        h_f[pl.ds(r0, SUBLANES), :] = h_f[pl.ds(r0, SUBLANES), :] + hh
        return hh[0:1]

    lax.fori_loop(0, nblk, bwd, jnp.zeros((1, HEAD), F32))
    g = g_ref[0]
    gelu = 0.5 * g * (1.0 + jnp.tanh(math.sqrt(2.0 / math.pi) * (g + 0.044715 * (g * g * g))))
    o_ref[0] = h_f[...] * gelu


def _rglru_call(proj, cw, cb, wg, bg, c8):
    bn, L, _ = proj.shape
    nt = C_WIDTH // HEAD
    blk = (1, L, HEAD)
    scr = pltpu.VMEM((L, HEAD), F32)
    return pl.pallas_call(
        functools.partial(_rglru_kernel, L=L),
        grid=(bn, nt),
        in_specs=[
            pl.BlockSpec(blk, lambda b, j: (b, 0, COL_CX // HEAD + j)),
            pl.BlockSpec(blk, lambda b, j: (b, 0, COL_CG // HEAD + j)),
            pl.BlockSpec((4, HEAD), lambda b, j: (0, j)),
            pl.BlockSpec((1, HEAD), lambda b, j: (0, j)),
            pl.BlockSpec((1, HEAD, 4 * HEAD), lambda b, j: (j, 0, 0)),
            pl.BlockSpec((1, 4, HEAD), lambda b, j: (j, 0, 0)),
            pl.BlockSpec((1, 2, HEAD), lambda b, j: (j, 0, 0)),
        ],
        out_specs=pl.BlockSpec(blk, lambda b, j: (b, 0, j)),
        out_shape=jax.ShapeDtypeStruct((bn, L, C_WIDTH), F32),
        scratch_shapes=[pltpu.VMEM((L + 2 * SUBLANES, HEAD), F32), scr, scr, scr, scr, scr],
        compiler_params=_cparams(("arbitrary", "arbitrary")),
    )(proj, proj, cw, cb, wg, bg, c8)


def _out_kernel(oa_ref, ob_ref, yc_ref, x_ref, gm_ref, rgn_ref, w_ref, npost_ref, npre_ref,
                sc_ref, sh_ref, wr_ref, x1_ref, h2_ref, pt_ref):
    a = oa_ref[...].astype(BF16)
    b = ob_ref[...].astype(BF16)
    c = (_rms(yc_ref[...], 1e-6) * rgn_ref[...]).astype(BF16)
    mix = jnp.dot(a, w_ref[0:A_WIDTH, :], preferred_element_type=F32)
    mix = mix + jnp.dot(b, w_ref[A_WIDTH:A_WIDTH + B_WIDTH, :], preferred_element_type=F32)
    mix = mix + jnp.dot(c, w_ref[A_WIDTH + B_WIDTH:, :], preferred_element_type=F32)
    x1 = x_ref[...] + gm_ref[0] * (_rms(mix, 1e-6) * npost_ref[...])
    x1_ref[...] = x1
    h2 = _rms(x1, 1e-6) * npre_ref[...]
    h2 = h2 * (1.0 + sc_ref[0]) + sh_ref[0]
    h2_ref[...] = h2
    logits = _dot_nt(wr_ref[...], h2, precision=HIGHEST)
    m = jnp.max(logits, axis=0, keepdims=True)
    e = jnp.exp(logits - m)
    pt_ref[...] = e / jnp.sum(e, axis=0, keepdims=True)


def _out_call(oa, ob, yc, x, gm, rgn, w_bf16, npost, npre, sc, sh, wr_t, L):
    M = x.shape[0]
    tm = 256
    lt = L // tm
    row = lambda w: pl.BlockSpec((tm, w), lambda i: (i, 0))
    vec = lambda w: pl.BlockSpec((1, w), lambda i: (0, 0))
    per_b = pl.BlockSpec((1, 1, D_MODEL), lambda i: (i // lt, 0, 0))
    return pl.pallas_call(
        _out_kernel,
        grid=(M // tm,),
        in_specs=[
            row(A_WIDTH), row(B_WIDTH), row(C_WIDTH), row(D_MODEL),
            per_b, vec(C_WIDTH),
            pl.BlockSpec((D_MODEL, D_MODEL), lambda i: (0, 0)),
            vec(D_MODEL), vec(D_MODEL), per_b, per_b,
            pl.BlockSpec((N_EXPERTS, D_MODEL), lambda i: (0, 0)),
        ],
        out_specs=[row(D_MODEL), row(D_MODEL), pl.BlockSpec((N_EXPERTS, tm), lambda i: (0, i))],
        out_shape=[
            jax.ShapeDtypeStruct((M, D_MODEL), F32),
            jax.ShapeDtypeStruct((M, D_MODEL), F32),
            jax.ShapeDtypeStruct((N_EXPERTS, M), F32),
        ],
        compiler_params=_cparams(("arbitrary",)),
    )(oa, ob, yc, x, gm, rgn.reshape(1, C_WIDTH), w_bf16, npost.reshape(1, D_MODEL),
      npre.reshape(1, D_MODEL), sc, sh, wr_t)


def _select_kernel(p_ref, sel_ref, *, cap, n):
    bits = lax.bitcast_convert_type(p_ref[...], I32)
    idx = lax.broadcasted_iota(I32, bits.shape, 1)

    def count(mask):
        return jnp.sum(mask.astype(F32), axis=-1, keepdims=True).astype(I32)

    def value_step(i, ans):
        cand = ans | lax.shift_left(jnp.int32(1), 30 - i)
        return jnp.where(count(bits >= cand) >= cap, cand, ans)

    thr = lax.fori_loop(0, 31, value_step, jnp.zeros((N_EXPERTS, 1), I32))
    gt = bits > thr
    eq = bits == thr
    need = cap - count(gt)

    def index_step(i, lohi):
        lo, hi = lohi
        mid = lax.shift_right_arithmetic(lo + hi, 1)
        ok = count(eq & (idx <= mid)) >= need
        return jnp.where(ok, lo, mid + 1), jnp.where(ok, mid, hi)

    steps = max(1, (n - 1).bit_length())
    lo, _ = lax.fori_loop(0, steps, index_step,
                          (jnp.zeros((N_EXPERTS, 1), I32), jnp.full((N_EXPERTS, 1), n - 1, I32)))
    sel_ref[...] = (gt | (eq & (idx <= lo))).astype(I32)


def _select_call(probs_t, cap):
    n = probs_t.shape[1]
    return pl.pallas_call(
        functools.partial(_select_kernel, cap=cap, n=n),
        out_shape=jax.ShapeDtypeStruct((N_EXPERTS, n), I32),
        compiler_params=pltpu.CompilerParams(vmem_limit_bytes=VMEM_LIMIT),
    )(probs_t)


def _compact_kernel(m_ref, p0_ref, p1_ref, excl_ref, c0_ref, c1_ref, *, rows, fill0, fill1):
    mask = m_ref[...]
    mask_f = mask.astype(F32)
    li = lax.broadcasted_iota(I32, (LANES, LANES), 0)
    lj = lax.broadcasted_iota(I32, (LANES, LANES), 1)
    upper = (li <= lj).astype(BF16)
    c_row = jnp.dot(mask.astype(BF16), upper, preferred_element_type=F32)
    rb = min(rows, 256)
    ri = lax.broadcasted_iota(I32, (rb, rb), 0)
    rj = lax.broadcasted_iota(I32, (rb, rb), 1)
    strict = (rj < ri).astype(BF16)
    carry = jnp.zeros((1, LANES), F32)
    offs = []
    for blk in range(rows // rb):
        tot = jnp.broadcast_to(c_row[blk * rb:(blk + 1) * rb, LANES - 1:LANES], (rb, LANES))
        pre = jnp.dot(strict, tot.astype(BF16), preferred_element_type=F32) + carry
        offs.append(pre)
        carry = pre[rb - 1:rb] + tot[rb - 1:rb]
    row_off = offs[0] if len(offs) == 1 else jnp.concatenate(offs, axis=0)
    excl = (row_off + c_row - mask_f).astype(I32)
    excl_ref[...] = excl

    lane = lax.broadcasted_iota(I32, (rows, LANES), 1)
    flat = lax.broadcasted_iota(I32, (rows, LANES), 0) * LANES + lane
    valid = mask
    disp = jnp.where(mask != 0, flat - excl, 0)
    pay0 = p0_ref[...]
    pay1 = p1_ref[...]
    nbits = (rows * LANES - 1).bit_length()
    for bit in range(nbits):
        s = 1 << bit
        if s < LANES:
            def shift(x, s=s):
                t = pltpu.roll(x, LANES - s, 1)
                t2 = pltpu.roll(t, rows - 1, 0)
                return jnp.where(lane < LANES - s, t, t2)
        else:
            def shift(x, s=s):
                return pltpu.roll(x, rows - s // LANES, 0)
        moving = valid & (lax.shift_right_logical(disp, bit) & 1)
        arrive = shift(moving) != 0
        disp = jnp.where(arrive, shift(disp), disp)
        pay0 = jnp.where(arrive, shift(pay0), pay0)
        pay1 = jnp.where(arrive, shift(pay1), pay1)
        valid = jnp.where(arrive, 1, valid & (1 - moving))
    c0_ref[...] = jnp.where(valid != 0, pay0, fill0)
    c1_ref[...] = jnp.where(valid != 0, pay1, fill1)


def _compact_call(mask, pay0, pay1, fill0, fill1):
    rows = mask.shape[0]
    out = jax.ShapeDtypeStruct((rows, LANES), I32)
    return pl.pallas_call(
        functools.partial(_compact_kernel, rows=rows, fill0=fill0, fill1=fill1),
        out_shape=[out, out, out],
        compiler_params=pltpu.CompilerParams(vmem_limit_bytes=VMEM_LIMIT),
    )(mask, pay0, pay1)


def _ffn_kernel(idx_ref, idx_next_ref, h_hbm, gate_ref, wg_ref, wu_ref, wd_ref, o_ref, buf, sem,
                *, tm, nsteps, nblk):
    step = pl.program_id(0) * nblk + pl.program_id(1)
    slot = lax.rem(step, 2)

    def issue(ref, dst_slot):
        def body(r, carry):
            tok = ref[0, 0, r]
            pltpu.make_async_copy(h_hbm.at[pl.ds(tok, 1), :], buf.at[dst_slot, pl.ds(r, 1), :],
                                  sem.at[dst_slot]).start()
            return carry
        lax.fori_loop(0, tm, body, 0)

    @pl.when(step == 0)
    def _():
        issue(idx_ref, 0)

    @pl.when(step + 1 < nsteps)
    def _():
        issue(idx_next_ref, 1 - slot)

    pltpu.make_async_copy(h_hbm.at[pl.ds(0, tm), :], buf.at[slot], sem.at[slot]).wait()
    x = buf[slot].astype(BF16)
    hg = jnp.dot(x, wg_ref[0], preferred_element_type=F32)
    hu = jnp.dot(x, wu_ref[0], preferred_element_type=F32)
    hid = (hg * _sigmoid(hg) * hu).astype(BF16)
    y = jnp.dot(hid, wd_ref[0], preferred_element_type=F32)
    o_ref[...] = y * gate_ref[...]


def _ffn_call(idx, h2, gates, wg, wu, wd, tm):
    n_e, slots = idx.shape
    nblk = slots // tm
    nsteps = n_e * nblk
    idx3 = idx.reshape(nsteps, 1, tm)

    def nxt(e, j):
        lin = jnp.minimum(e * nblk + j + 1, nsteps - 1)
        return (lin, 0, 0)

    return pl.pallas_call(
        functools.partial(_ffn_kernel, tm=tm, nsteps=nsteps, nblk=nblk),
        grid=(n_e, nblk),
        in_specs=[
            pl.BlockSpec((1, 1, tm), lambda e, j: (e * nblk + j, 0, 0), memory_space=pltpu.SMEM),
            pl.BlockSpec((1, 1, tm), nxt, memory_space=pltpu.SMEM),
            pl.BlockSpec(memory_space=pl.ANY),
            pl.BlockSpec((tm, 1), lambda e, j: (e * nblk + j, 0)),
            pl.BlockSpec((1, D_MODEL, D_EXPERT), lambda e, j: (e, 0, 0)),
            pl.BlockSpec((1, D_MODEL, D_EXPERT), lambda e, j: (e, 0, 0)),
            pl.BlockSpec((1, D_EXPERT, D_MODEL), lambda e, j: (e, 0, 0)),
        ],
        out_specs=pl.BlockSpec((tm, D_MODEL), lambda e, j: (e * nblk + j, 0)),
        out_shape=jax.ShapeDtypeStruct((n_e * slots, D_MODEL), F32),
        scratch_shapes=[pltpu.VMEM((2, tm, D_MODEL), F32), pltpu.SemaphoreType.DMA((2,))],
        compiler_params=_cparams(("arbitrary", "arbitrary")),
    )(idx3, idx3, h2, gates, wg, wu, wd)


def _combine_kernel(off_ref, src_hbm, tok_hbm, ye_hbm, x1_ref, gf_ref, nw_ref, o_ref,
                    src_smem, tok_v, zbuf, acc, sem, *, tt, win):
    i = pl.program_id(0)
    p0 = off_ref[i]
    p1 = off_ref[i + 1]
    p0a = (p0 // LANES) * LANES
    nch = (p1 - p0a + win - 1) // win
    acc[...] = jnp.zeros_like(acc)
    tok_col = i * tt + lax.broadcasted_iota(I32, (tt, 1), 0)

    def chunk(c, carry):
        ps = pl.multiple_of(p0a + c * win, LANES)
        cp_src = pltpu.make_async_copy(src_hbm.at[pl.ds(ps, win)], src_smem, sem.at[0])
        cp_tok = pltpu.make_async_copy(tok_hbm.at[pl.ds(ps // LANES, win // LANES), :], tok_v, sem.at[1])
        cp_src.start()
        cp_tok.start()
        cp_src.wait()

        def row(r, cc):
            s = src_smem[r]
            pltpu.make_async_copy(ye_hbm.at[pl.ds(s, 1), :], zbuf.at[pl.ds(r, 1), :], sem.at[2]).start()
            return cc
        lax.fori_loop(0, win, row, 0)
        cp_tok.wait()
        pltpu.make_async_copy(ye_hbm.at[pl.ds(0, win), :], zbuf, sem.at[2]).wait()
        z = zbuf[...]
        z_hi = z.astype(BF16)
        z_lo = (z - z_hi.astype(F32)).astype(BF16)
        upd = jnp.zeros(acc.shape, F32)
        for kk in range(win // LANES):
            seg = (tok_v[kk:kk + 1, :] == tok_col).astype(BF16)
            rows = slice(kk * LANES, (kk + 1) * LANES)
            upd = upd + jnp.dot(seg, z_hi[rows], preferred_element_type=F32)
            upd = upd + jnp.dot(seg, z_lo[rows], preferred_element_type=F32)
        acc[...] = acc[...] + upd
        return carry

    lax.fori_loop(0, nch, chunk, 0)
    y = acc[...]
    o_ref[...] = x1_ref[...] + gf_ref[0] * (_rms(y, 1e-6) * nw_ref[...])


def _combine_call(off, src, tok, ye, x1, gf, nw, L, tt, win):
    M = x1.shape[0]
    lt = L // tt
    grid_spec = pltpu.PrefetchScalarGridSpec(
        num_scalar_prefetch=1,
        grid=(M // tt,),
        in_specs=[
            pl.BlockSpec(memory_space=pl.ANY),
            pl.BlockSpec(memory_space=pl.ANY),
            pl.BlockSpec(memory_space=pl.ANY),
            pl.BlockSpec((tt, D_MODEL), lambda i, off: (i, 0)),
            pl.BlockSpec((1, 1, D_MODEL), lambda i, off: (i // lt, 0, 0)),
            pl.BlockSpec((1, D_MODEL), lambda i, off: (0, 0)),
        ],
        out_specs=pl.BlockSpec((tt, D_MODEL), lambda i, off: (i, 0)),
        scratch_shapes=[
            pltpu.SMEM((win,), I32),
            pltpu.VMEM((win // LANES, LANES), I32),
            pltpu.VMEM((win, D_MODEL), F32),
            pltpu.VMEM((tt, D_MODEL), F32),
            pltpu.SemaphoreType.DMA((3,)),
        ],
    )
    return pl.pallas_call(
        functools.partial(_combine_kernel, tt=tt, win=win),
        grid_spec=grid_spec,
        out_shape=jax.ShapeDtypeStruct((M, D_MODEL), F32),
        compiler_params=_cparams(("arbitrary",)),
    )(off, src, tok, ye, x1, gf, nw.reshape(1, D_MODEL))


def _route_group(probs_t, tok_base, slot_base, slots_total, tt):
    n = probs_t.shape[1]
    cap = max(1, EC_FACTOR * n // N_EXPERTS)
    rows = N_EXPERTS * n // LANES
    sel = _select_call(probs_t, cap)
    tok_ids = tok_base + lax.broadcasted_iota(I32, (N_EXPERTS, n), 1)
    excl_e, idx_c, gate_c = _compact_call(
        sel.reshape(rows, LANES), tok_ids.reshape(rows, LANES),
        lax.bitcast_convert_type(probs_t, I32).reshape(rows, LANES), tok_base, 0)
    npair = N_EXPERTS * cap
    idx_e = idx_c.reshape(-1)[:npair].reshape(N_EXPERTS, cap)
    gate_e = lax.bitcast_convert_type(gate_c.reshape(-1)[:npair], F32).reshape(N_EXPERTS, cap)
    e_col = lax.broadcasted_iota(I32, (N_EXPERTS, n), 0)
    src = excl_e.reshape(N_EXPERTS, n) - e_col * cap + e_col * slots_total + slot_base
    excl_t, src_c, tok_c = _compact_call(
        sel.T.reshape(rows, LANES), src.T.reshape(rows, LANES),
        tok_ids.T.reshape(rows, LANES), 0, -1)
    src_t = src_c.reshape(-1)[:npair]
    tok_t = tok_c.reshape(-1)[:npair]
    off = excl_t.reshape(-1)[::tt * N_EXPERTS]
    return idx_e, gate_e, src_t, tok_t, off, npair


def _block_diag_tiles(w):
    nt = C_WIDTH // HEAD
    per = HEAD // C_BLOCK
    w = w.reshape(2, nt, per, C_BLOCK, C_BLOCK)
    eye = jnp.eye(per, dtype=w.dtype)
    t = jnp.einsum('dtpce,pq->dtpcqe', w, eye)
    return t.reshape(2, nt, HEAD, HEAD).transpose(1, 0, 2, 3)


def kernel(x_prompt, x_sample, c_prompt, c_sample, ada_w, ada_b, norm_mix_pre, norm_mix_post, norm_ffn_pre, norm_ffn_post, w_in, hg_lower, hg_norm, dl_q1, dl_k1, dl_q2, dl_k2, dl_subln, conv_w, conv_b, rg_wa, rg_ba, rg_wx, rg_bx, rg_lambda, rg_norm, w_out, w_router, w_gate, w_up, w_down):
    bp, L, D = x_prompt.shape
    bs = x_sample.shape[0]
    bn = bp + bs
    n_p, n_s = bp * L, bs * x_sample.shape[1]
    M = n_p + n_s
    x = jnp.concatenate([x_prompt, x_sample], axis=0)
    c = jnp.concatenate([c_prompt, c_sample], axis=0)
    mod = _ada_call(c, ada_w, ada_b)

    lb_soft = jax.nn.softmax(hg_lower.astype(F32), axis=0)
    lb_all = jnp.cumsum(lb_soft, axis=0) - lb_soft[0:1]
    half = ROT_DIM // 2
    inv_freq = ROPE_THETA ** (-jnp.arange(half, dtype=F32) / half)
    ang = jnp.arange(L, dtype=F32)[:, None] * inv_freq[None, :]
    cos, sin = jnp.cos(ang), jnp.sin(ang)
    one = jnp.ones((L, B_DQK - ROT_DIM), F32)
    zero = jnp.zeros((L, B_DQK - ROT_DIM), F32)
    zh = jnp.zeros((L, half), F32)
    cos_t = jnp.tile(jnp.concatenate([cos, cos, one], axis=1), (1, 2))
    s1_t = jnp.tile(jnp.concatenate([-sin, zh, zero], axis=1), (1, 2))
    s2_t = jnp.tile(jnp.concatenate([zh, sin, zero], axis=1), (1, 2))

    cap_p = max(1, EC_FACTOR * n_p // N_EXPERTS)
    cap_s = max(1, EC_FACTOR * n_s // N_EXPERTS)
    slots_total = cap_p + cap_s
    tm_ffn = math.gcd(256, math.gcd(cap_p, cap_s))
    tt = 256 if L % 256 == 0 else L
    win = 256

    xf = x
    for l in range(DEPTH):
        m6 = mod[l].reshape(bn, N_MOD, 1, D)
        sh_m, sc_m, g_m, sh_f, sc_f, g_f = (m6[:, i] for i in range(N_MOD))

        proj = _in_call(xf, norm_mix_pre[l], sc_m, sh_m, w_in[l].astype(BF16))

        lb = lb_all[l].reshape(A_HEADS, HEAD)
        par = jnp.stack([jnp.log(jnp.maximum(lb, LB_MIN)), jnp.log1p(-lb), 1.0 - lb], axis=1)
        o_a = _hgrn_call(proj, par, hg_norm[l])

        lam_init = 0.8 - 0.6 * math.exp(-0.3 * l)
        lam = (jnp.exp(jnp.sum(dl_q1[l].astype(F32) * dl_k1[l].astype(F32)))
               - jnp.exp(jnp.sum(dl_q2[l].astype(F32) * dl_k2[l].astype(F32))) + lam_init)
        lam2 = jnp.stack([lam, jnp.asarray(1.0 - lam_init, F32)])
        qb, kb, vb = _qkv_call(proj, cos_t, s1_t, s2_t)
        o_b = _attn_call(lam2, qb, kb, vb, dl_subln[l])

        nt = C_WIDTH // HEAD
        wa_t = _block_diag_tiles(rg_wa[l])
        wx_t = _block_diag_tiles(rg_wx[l])
        wg = jnp.concatenate([wa_t[:, 0], wx_t[:, 0], wa_t[:, 1], wx_t[:, 1]], axis=-1).astype(BF16)
        bg = jnp.stack([rg_ba[l, 0], rg_bx[l, 0], rg_ba[l, 1], rg_bx[l, 1]], axis=0)
        bg = bg.reshape(4, nt, HEAD).transpose(1, 0, 2)
        c8 = (-RG_C * jax.nn.softplus(-rg_lambda[l])).reshape(2, nt, HEAD).transpose(1, 0, 2)
        y_c = _rglru_call(proj, conv_w[l], conv_b[l].reshape(1, C_WIDTH), wg, bg, c8)

        x1, h2, probs_t = _out_call(
            o_a.reshape(M, A_WIDTH), o_b.reshape(M, B_WIDTH), y_c.reshape(M, C_WIDTH),
            xf.reshape(M, D), g_m, rg_norm[l], w_out[l].astype(BF16), norm_mix_post[l],
            norm_ffn_pre[l], sc_f, sh_f, w_router[l].T, L)

        ie_p, ge_p, src_p, tok_p, off_p, np_p = _route_group(probs_t[:, :n_p], 0, 0, slots_total, tt)
        ie_s, ge_s, src_s, tok_s, off_s, np_s = _route_group(probs_t[:, n_p:], n_p, cap_p, slots_total, tt)
        idx = jnp.concatenate([ie_p, ie_s], axis=1)
        gates = jnp.concatenate([ge_p, ge_s], axis=1).reshape(N_EXPERTS * slots_total, 1)
        ye = _ffn_call(idx, h2, gates, w_gate[l].astype(BF16), w_up[l].astype(BF16),
                       w_down[l].astype(BF16), tm_ffn)

        pad = 2 * win
        src = jnp.concatenate([src_p, src_s, jnp.zeros((pad,), I32)])
        tok = jnp.concatenate([tok_p, tok_s, jnp.full((pad,), -1, I32)]).reshape(-1, LANES)
        off = jnp.concatenate([off_p, off_s + np_p, jnp.full((1,), np_p + np_s, I32)])
        x2 = _combine_call(off, src, tok, ye, x1, g_f, norm_ffn_post[l], L, tt, win)
        xf = x2.reshape(bn, L, D)

    return xf[:bp], xf[bp:]
```

```python
import functools
import math

import jax
import jax.numpy as jnp
from jax import lax
from jax.experimental import pallas as pl
from jax.experimental.pallas import tpu as pltpu

F32 = jnp.float32
BF16 = jnp.bfloat16
I32 = jnp.int32
U32 = jnp.uint32
HIGHEST = lax.Precision.HIGHEST

D_MODEL = 2048
DEPTH = 2
A_WIDTH = D_MODEL // 4
B_WIDTH = D_MODEL // 2
C_WIDTH = D_MODEL // 4
HEAD = 128
A_HEADS = A_WIDTH // HEAD
B_HEADS = B_WIDTH // HEAD
B_DQK = HEAD // 2
ROT_DIM = B_DQK // 4
ROPE_THETA = 500000.0
LB_MIN = 1e-12
C_BLOCKS = 8
C_BLOCK = C_WIDTH // C_BLOCKS
RG_C = 8.0
N_EXPERTS = 16
EC_FACTOR = 2
D_EXPERT = D_MODEL // 2
N_MOD = 6
IN_COLS = 3 * A_WIDTH + 2 * A_WIDTH + 3 * B_WIDTH + 2 * C_WIDTH
COL_AQ, COL_AFF, COL_AFB, COL_AI, COL_AG = 0, 512, 1024, 1536, 2048
COL_BQ, COL_BK, COL_BV = 2560, 3584, 4608
COL_CX, COL_CG = 5632, 6144

LANES = 128
SUBLANES = 8
VMEM_LIMIT = 56 * 1024 * 1024

HGRN_CHUNK = 64
HGRN_SUB = 16
HGRN_BLOCK = 256
HGRN_SAFE_RANGE = 80.0
ATTN_TQ = 1024
ATTN_SUB = 256


def _cparams(sem):
    return pltpu.CompilerParams(dimension_semantics=sem, vmem_limit_bytes=VMEM_LIMIT)


def _sigmoid(x):
    return 1.0 / (1.0 + jnp.exp(-x))


def _rms(x, eps):
    return x * lax.rsqrt(jnp.mean(x * x, axis=-1, keepdims=True) + eps)


def _dot_nt(a, b, **kw):
    return lax.dot_general(a, b, (((1,), (1,)), ((), ())), preferred_element_type=F32, **kw)


def _dot_tn(a, b):
    return lax.dot_general(a, b, (((0,), (0,)), ((), ())), preferred_element_type=F32)


def _ada_kernel(c_ref, w_ref, b_ref, o_ref):
    c = c_ref[...]
    a = (c * _sigmoid(c)).astype(BF16)
    o_ref[0] = jnp.dot(a, w_ref[0].astype(BF16), preferred_element_type=F32) + b_ref[0]


def _ada_call(c, ada_w, ada_b):
    bn = c.shape[0]
    tn = 1024
    ncol = N_MOD * D_MODEL
    return pl.pallas_call(
        _ada_kernel,
        grid=(DEPTH, ncol // tn),
        in_specs=[
            pl.BlockSpec((bn, D_MODEL), lambda l, j: (0, 0)),
            pl.BlockSpec((1, D_MODEL, tn), lambda l, j: (l, 0, j)),
            pl.BlockSpec((1, 1, tn), lambda l, j: (l, 0, j)),
        ],
        out_specs=pl.BlockSpec((1, bn, tn), lambda l, j: (l, 0, j)),
        out_shape=jax.ShapeDtypeStruct((DEPTH, bn, ncol), F32),
        name="ada_mod",
        compiler_params=_cparams(("arbitrary", "arbitrary")),
    )(c, ada_w, ada_b.reshape(DEPTH, 1, ncol))


def _in_kernel(x_ref, nw_ref, sc_ref, sh_ref, w_ref, o_ref, h_scr):
    @pl.when(pl.program_id(1) == 0)
    def _():
        h = _rms(x_ref[0], 1e-6) * nw_ref[...]
        h = h * (1.0 + sc_ref[0]) + sh_ref[0]
        h_scr[...] = h.astype(BF16)

    o_ref[0] = jnp.dot(h_scr[...], w_ref[...], preferred_element_type=F32)


def _in_call(x, nw, sc, sh, w_bf16):
    bn, L, _ = x.shape
    tm = min(1024, L)
    tn = 512
    lt = L // tm
    return pl.pallas_call(
        _in_kernel,
        grid=(bn * lt, IN_COLS // tn),
        in_specs=[
            pl.BlockSpec((1, tm, D_MODEL), lambda i, j: (i // lt, i % lt, 0)),
            pl.BlockSpec((1, D_MODEL), lambda i, j: (0, 0)),
            pl.BlockSpec((1, 1, D_MODEL), lambda i, j: (i // lt, 0, 0)),
            pl.BlockSpec((1, 1, D_MODEL), lambda i, j: (i // lt, 0, 0)),
            pl.BlockSpec((D_MODEL, tn), lambda i, j: (0, j)),
        ],
        out_specs=pl.BlockSpec((1, tm, tn), lambda i, j: (i // lt, i % lt, j)),
        out_shape=jax.ShapeDtypeStruct((bn, L, IN_COLS), F32),
        scratch_shapes=[pltpu.VMEM((tm, D_MODEL), BF16)],
        name="in_proj",
        compiler_params=_cparams(("arbitrary", "arbitrary")),
    )(x, nw.reshape(1, D_MODEL), sc, sh, w_bf16)


def _sigmoid_pair(z):
    e = jnp.exp(-jnp.abs(z))
    r = 1.0 / (1.0 + e)
    er = e * r
    pos = z >= 0.0
    return jnp.where(pos, r, er), jnp.where(pos, er, r)


def _split3(x):
    hi = x.astype(BF16)
    r = x - hi.astype(F32)
    mid = r.astype(BF16)
    lo = (r - mid.astype(F32)).astype(BF16)
    return hi, mid, lo


def _hgrn_kernel(q_ref, ff_ref, fb_ref, i_ref, g_ref, par_ref, nw_ref, o_ref,
                 qs_scr, b_scr, k_scr, acc_scr, qe_scr, u_scr, dec_scr, st_scr, *, L):
    C, SB = HGRN_CHUNK, HGRN_SUB
    BLK = min(HGRN_BLOCK, L)
    cpb = BLK // C
    nblk = L // BLK
    nchunk = L // C
    nsb = C // SB
    lb_floor = par_ref[0, 0:1, :]
    one_m_lb = par_ref[0, 1:2, :]
    brow = lax.broadcasted_iota(I32, (BLK, BLK), 0)
    bcol = lax.broadcasted_iota(I32, (BLK, BLK), 1)
    same_chunk = (brow // C) == (bcol // C)
    sub_row = lax.broadcasted_iota(I32, (SB, 1), 0)

    ql = q_ref[0]
    qs_scr[...] = ql * _sigmoid_pair(ql)[0]

    def run_dir(f_ref, rev, first):
        causal = same_chunk & ((bcol >= brow) if rev else (bcol <= brow))
        tri = causal.astype(BF16)
        mid_off = C // 2
        end_off = 0 if rev else C - 1
        beg_off = C - 1 if rev else 0

        def gates(bi, rng):
            r0 = pl.multiple_of(bi * BLK, BLK)
            z = f_ref[0, pl.ds(r0, BLK), :]
            sig, sig_neg = _sigmoid_pair(z)
            log_f = jnp.log(lb_floor + one_m_lb * sig)
            k_scr[pl.ds(r0, BLK), :] = one_m_lb * sig_neg
            hi, mid, lo = _split3(log_f)
            b = (jnp.dot(tri, hi, preferred_element_type=F32)
                 + jnp.dot(tri, mid, preferred_element_type=F32)
                 + jnp.dot(tri, lo, preferred_element_type=F32))
            b_scr[pl.ds(r0, BLK), :] = b
            for c in range(cpb):
                m = b[c * C + mid_off:c * C + mid_off + 1]
                rng = jnp.maximum(rng, b[c * C + beg_off:c * C + beg_off + 1] - m)
                rng = jnp.maximum(rng, m - b[c * C + end_off:c * C + end_off + 1])
            return rng

        rng = lax.fori_loop(0, nblk, gates, jnp.zeros((1, HEAD), F32), unroll=2)
        safe = jnp.max(rng) <= HGRN_SAFE_RANGE

        @pl.when(safe)
        def _():
            def intra(bi, carry):
                r0 = pl.multiple_of(bi * BLK, BLK)
                b = b_scr[pl.ds(r0, BLK), :]
                k = k_scr[pl.ds(r0, BLK), :]
                q = qs_scr[pl.ds(r0, BLK), :]
                v_bf = i_ref[0, pl.ds(r0, BLK), :].astype(BF16)
                m = jnp.concatenate(
                    [jnp.broadcast_to(b[c * C + mid_off:c * C + mid_off + 1], (C, HEAD)) for c in range(cpb)], axis=0)
                b_end = jnp.concatenate(
                    [jnp.broadcast_to(b[c * C + end_off:c * C + end_off + 1], (C, HEAD)) for c in range(cpb)], axis=0)
                qt = (q * jnp.exp(b - m)).astype(BF16)
                kt = (k * jnp.exp(m - b)).astype(BF16)
                sc = _dot_nt(qt, kt)
                sc = jnp.where(causal, sc, 0.0).astype(BF16)
                o_intra = jnp.dot(sc, v_bf, preferred_element_type=F32)
                if first:
                    acc_scr[pl.ds(r0, BLK), :] = o_intra
                else:
                    acc_scr[pl.ds(r0, BLK), :] = acc_scr[pl.ds(r0, BLK), :] + o_intra
                qe_scr[pl.ds(r0, BLK), :] = (q * jnp.exp(b)).astype(BF16)
                kend = (k * jnp.exp(b_end - b)).astype(BF16)
                for c in range(cpb):
                    rows = slice(c * C, (c + 1) * C)
                    u_scr[bi * cpb + c] = _dot_tn(v_bf[rows], kend[rows])
                    dec_scr[pl.ds(bi * cpb + c, 1), :] = jnp.exp(b[c * C + end_off:c * C + end_off + 1])
                return carry

            lax.fori_loop(0, nblk, intra, 0, unroll=2)

            def inter(ci, st):
                c = (nchunk - 1 - ci) if rev else ci
                r0 = pl.multiple_of(c * C, C)
                o_state = _dot_nt(qe_scr[pl.ds(r0, C), :], st.astype(BF16))
                acc_scr[pl.ds(r0, C), :] = acc_scr[pl.ds(r0, C), :] + o_state
                return st * dec_scr[pl.ds(c, 1), :] + u_scr[c]

            lax.fori_loop(0, nchunk, inter, jnp.zeros((HEAD, HEAD), F32), unroll=4)

        @pl.when(jnp.logical_not(safe))
        def _():
            order = list(range(nsb - 1, -1, -1)) if rev else list(range(nsb))
            st_scr[...] = jnp.zeros_like(st_scr)

            def body(ci, carry):
                c = (nchunk - 1 - ci) if rev else ci
                r0 = pl.multiple_of(c * C, C)
                b = b_scr[pl.ds(r0, C), :]
                k = k_scr[pl.ds(r0, C), :]
                q = qs_scr[pl.ds(r0, C), :]
                v = i_ref[0, pl.ds(r0, C), :]
                st = st_scr[...]
                o_state = _dot_nt((q * jnp.exp(b)).astype(BF16), st.astype(BF16))
                v_bf = v.astype(BF16)
                for p, blk in enumerate(order):
                    lo = SB * blk
                    b_blk = b[lo:lo + SB]
                    q_blk = q[lo:lo + SB]
                    k_blk = k[lo:lo + SB]
                    v_blk = v[lo:lo + SB]
                    out = o_state[lo:lo + SB]
                    if p > 0:
                        if rev:
                            bound = b[lo + SB:lo + SB + 1]
                            e0, e1 = lo + SB, C
                        else:
                            bound = b[lo - 1:lo]
                            e0, e1 = 0, lo
                        qt = (q_blk * jnp.exp(b_blk - bound)).astype(BF16)
                        kt = (k[e0:e1] * jnp.exp(bound - b[e0:e1])).astype(BF16)
                        sc = _dot_nt(qt, kt)
                        out = out + jnp.dot(sc.astype(BF16), v_bf[e0:e1], preferred_element_type=F32)
                    diag = jnp.zeros((SB, HEAD), F32)
                    for t in range(SB):
                        bt = b_blk[t:t + 1]
                        pm = k_blk * jnp.exp(jnp.minimum(bt - b_blk, 0.0)) * q_blk[t:t + 1]
                        s = jnp.sum(pm, axis=-1, keepdims=True)
                        keep = (sub_row >= t) if rev else (sub_row <= t)
                        s = jnp.where(keep, s, 0.0)
                        o_t = jnp.sum(s * v_blk, axis=0, keepdims=True)
                        diag = jnp.where(sub_row == t, o_t, diag)
                    out = out + diag
                    rows = pl.ds(r0 + lo, SB)
                    if first:
                        acc_scr[rows, :] = out
                    else:
                        acc_scr[rows, :] = acc_scr[rows, :] + out
                b_end = b[0:1] if rev else b[C - 1:C]
                kend = (k * jnp.exp(b_end - b)).astype(BF16)
                st_scr[...] = st * jnp.exp(b_end) + _dot_tn(v_bf, kend)
                return carry

            lax.fori_loop(0, nchunk, body, 0)

    run_dir(ff_ref, False, True)
    run_dir(fb_ref, True, False)
    g = g_ref[0]
    o_ref[0] = (_rms(acc_scr[...], 1e-6) * nw_ref[...] * (g * _sigmoid(g))).astype(o_ref.dtype)


def _hgrn_call(proj, par, nw):
    bn, L, _ = proj.shape
    cb = lambda off: (lambda b, h: (b, 0, off // HEAD + h))
    blk = (1, L, HEAD)
    nchunk = L // HGRN_CHUNK
    seq = pltpu.VMEM((L, HEAD), F32)
    return pl.pallas_call(
        functools.partial(_hgrn_kernel, L=L),
        grid=(bn, A_HEADS),
        in_specs=[
            pl.BlockSpec(blk, cb(COL_AQ)),
            pl.BlockSpec(blk, cb(COL_AFF)),
            pl.BlockSpec(blk, cb(COL_AFB)),
            pl.BlockSpec(blk, cb(COL_AI)),
            pl.BlockSpec(blk, cb(COL_AG)),
            pl.BlockSpec((1, 2, HEAD), lambda b, h: (h, 0, 0)),
            pl.BlockSpec((1, HEAD), lambda b, h: (0, 0)),
        ],
        out_specs=pl.BlockSpec(blk, lambda b, h: (b, 0, h)),
        out_shape=jax.ShapeDtypeStruct((bn, L, A_WIDTH), BF16),
        scratch_shapes=[
            seq, seq, seq, seq,
            pltpu.VMEM((L, HEAD), BF16),
            pltpu.VMEM((nchunk, HEAD, HEAD), F32),
            pltpu.VMEM((nchunk, HEAD), F32),
            pltpu.VMEM((HEAD, HEAD), F32),
        ],
        name="hgrn2",
        compiler_params=_cparams(("arbitrary", "arbitrary")),
    )(proj, proj, proj, proj, proj, par, nw.reshape(1, HEAD))


def _qkv_kernel(q_ref, k_ref, v_ref, cos_ref, s1_ref, s2_ref, qo_ref, ko_ref, vo_ref):
    cos = cos_ref[...]
    s1 = s1_ref[...]
    s2 = s2_ref[...]
    scale = B_DQK ** -0.5 * math.log2(math.e)
    for h in range(q_ref.shape[2] // HEAD):
        sl = slice(HEAD * h, HEAD * (h + 1))
        for src, dst, mul in ((q_ref, qo_ref, scale), (k_ref, ko_ref, 1.0)):
            x = src[0, :, sl]
            xr = x * cos + pltpu.roll(x, HEAD - ROT_DIM // 2, 1) * s1 + pltpu.roll(x, ROT_DIM // 2, 1) * s2
            dst[0, :, sl] = (xr * mul).astype(BF16)
    vo_ref[0] = v_ref[0].astype(BF16)


def _qkv_call(proj, cos_t, s1_t, s2_t):
    bn, L, _ = proj.shape
    tl = min(512, L)
    wb = 512
    nj = B_WIDTH // wb
    cb = lambda off: (lambda b, i, j: (b, i, off // wb + j))
    out = jax.ShapeDtypeStruct((bn, L, B_WIDTH), BF16)
    ospec = pl.BlockSpec((1, tl, wb), lambda b, i, j: (b, i, j))
    tspec = pl.BlockSpec((tl, HEAD), lambda b, i, j: (i, 0))
    return pl.pallas_call(
        _qkv_kernel,
        grid=(bn, L // tl, nj),
        in_specs=[
            pl.BlockSpec((1, tl, wb), cb(COL_BQ)),
            pl.BlockSpec((1, tl, wb), cb(COL_BK)),
            pl.BlockSpec((1, tl, wb), cb(COL_BV)),
            tspec, tspec, tspec,
        ],
        out_specs=[ospec, ospec, ospec],
        out_shape=[out, out, out],
        name="rope_qkv",
        compiler_params=_cparams(("arbitrary", "arbitrary", "arbitrary")),
    )(proj, proj, proj, cos_t, s1_t, s2_t)


def _attn_kernel(lam_ref, q_ref, k_ref, v_ref, sw_ref, o_ref, vx_scr):
    @pl.when(pl.program_id(2) == 0)
    def _():
        lane = lax.broadcasted_iota(I32, (vx_scr.shape[0], HEAD), 1)
        vx_scr[:, 0:HEAD] = v_ref[0]
        vx_scr[:, HEAD:2 * HEAD] = jnp.where(lane == 0, 1.0, 0.0).astype(BF16)

    lam = lam_ref[0]
    post = lam_ref[1]
    k = k_ref[0]
    lane = lax.broadcasted_iota(I32, (1, HEAD), 1)
    sub = min(ATTN_SUB, q_ref.shape[1])
    nsub = q_ref.shape[1] // sub

    def scores(j):
        q = q_ref[0, j * sub:(j + 1) * sub, :]
        zero = jnp.zeros_like(q)
        return (_dot_nt(jnp.where(lane < B_DQK, q, zero), k),
                _dot_nt(jnp.where(lane >= B_DQK, q, zero), k))

    def weighted_values(s):
        m = jnp.max(s, axis=-1, keepdims=True)
        e = jnp.exp2((s - m).astype(BF16))
        ox = jnp.dot(e, vx_scr[...], preferred_element_type=F32)
        return ox[:, 0:HEAD], ox[:, HEAD:HEAD + 1]

    s_next = scores(0)
    for j in range(nsub):
        s1, s2 = s_next
        if j + 1 < nsub:
            s_next = scores(j + 1)
        o1, l1 = weighted_values(s1)
        o2, l2 = weighted_values(s2)
        o = o1 * (1.0 / l1) - o2 * (lam / l2)
        o_ref[0, j * sub:(j + 1) * sub, :] = (_rms(o, 1e-5) * sw_ref[...] * post).astype(o_ref.dtype)


def _attn_call(lam2, qb, kb, vb, sw):
    bn, L, _ = qb.shape
    tq = min(ATTN_TQ, L)
    return pl.pallas_call(
        _attn_kernel,
        grid=(bn, B_HEADS, L // tq),
        in_specs=[
            pl.BlockSpec(memory_space=pltpu.SMEM),
            pl.BlockSpec((1, tq, HEAD), lambda b, h, i: (b, i, h)),
            pl.BlockSpec((1, L, HEAD), lambda b, h, i: (b, 0, h)),
            pl.BlockSpec((1, L, HEAD), lambda b, h, i: (b, 0, h)),
            pl.BlockSpec((1, HEAD), lambda b, h, i: (0, 0)),
        ],
        out_specs=pl.BlockSpec((1, tq, HEAD), lambda b, h, i: (b, i, h)),
        out_shape=jax.ShapeDtypeStruct((bn, L, B_WIDTH), BF16),
        scratch_shapes=[pltpu.VMEM((L, 2 * HEAD), BF16)],
        name="diff_attn",
        compiler_params=_cparams(("arbitrary", "arbitrary", "arbitrary")),
    )(lam2, qb, kb, vb, sw.reshape(1, HEAD))


def _rglru_kernel(x_ref, g_ref, cw_ref, cb_ref, wg_ref, bg_ref, c8_ref, o_ref,
                  xs, a_f, x_f, a_b, x_b, h_f, *, L):
    pad = SUBLANES
    xs[0:pad, :] = jnp.zeros((pad, HEAD), F32)
    xs[pad + L:2 * pad + L, :] = jnp.zeros((pad, HEAD), F32)
    xs[pad:pad + L, :] = x_ref[0]
    cw = cw_ref[...]
    cb = cb_ref[...]
    wg = wg_ref[0]
    bg = bg_ref[0]
    c8 = c8_ref[0]
    tc = min(256, L)
    for ci in range(L // tc):
        r0 = ci * tc
        u = cb
        for j in range(4):
            u = u + xs[pad - 2 + j + r0:pad - 2 + j + r0 + tc, :] * cw[j:j + 1]
        gates = jnp.dot(u.astype(BF16), wg, preferred_element_type=F32)
        for d, (a_scr, x_scr) in enumerate(((a_f, x_f), (a_b, x_b))):
            r = _sigmoid(gates[:, (2 * d) * HEAD:(2 * d + 1) * HEAD] + bg[2 * d:2 * d + 1])
            ig = _sigmoid(gates[:, (2 * d + 1) * HEAD:(2 * d + 2) * HEAD] + bg[2 * d + 1:2 * d + 2])
            log_a = c8[d:d + 1] * r
            a = jnp.exp(log_a)
            a_scr[r0:r0 + tc, :] = a
            x_scr[r0:r0 + tc, :] = jnp.sqrt(1.0 - a * a) * (ig * u)

    rowi = lax.broadcasted_iota(I32, (SUBLANES, HEAD), 0)
    nblk = L // SUBLANES

    def fwd(i, h):
        r0 = pl.multiple_of(i * SUBLANES, SUBLANES)
        a = a_f[pl.ds(r0, SUBLANES), :]
        x = x_f[pl.ds(r0, SUBLANES), :]
        for s in (1, 2, 4):
            ok = rowi >= s
            a_s = jnp.where(ok, pltpu.roll(a, s, 0), 1.0)
            x_s = jnp.where(ok, pltpu.roll(x, s, 0), 0.0)
            x = a * x_s + x
            a = a * a_s
        hh = x + a * h
        h_f[pl.ds(r0, SUBLANES), :] = hh
        return hh[SUBLANES - 1:SUBLANES]

    lax.fori_loop(0, nblk, fwd, jnp.zeros((1, HEAD), F32))

    def bwd(i, h):
        r0 = pl.multiple_of((nblk - 1 - i) * SUBLANES, SUBLANES)
        a = a_b[pl.ds(r0, SUBLANES), :]
        x = x_b[pl.ds(r0, SUBLANES), :]
        for s in (1, 2, 4):
            ok = rowi < SUBLANES - s
            a_s = jnp.where(ok, pltpu.roll(a, SUBLANES - s, 0), 1.0)
            x_s = jnp.where(ok, pltpu.roll(x, SUBLANES - s, 0), 0.0)
            x = a * x_s + x
            a = a * a_s
        hh = x + a * h
        h_f[pl.ds(r0, SUBLANES), :] = h_f[pl.ds(r0, SUBLANES), :] + hh
        return hh[0:1]

    lax.fori_loop(0, nblk, bwd, jnp.zeros((1, HEAD), F32))
    g = g_ref[0]
    gelu = 0.5 * g * (1.0 + jnp.tanh(math.sqrt(2.0 / math.pi) * (g + 0.044715 * (g * g * g))))
    o_ref[0] = h_f[...] * gelu


def _rglru_call(proj, cw, cb, wg, bg, c8):
    bn, L, _ = proj.shape
    nt = C_WIDTH // HEAD
    blk = (1, L, HEAD)
    scr = pltpu.VMEM((L, HEAD), F32)
    return pl.pallas_call(
        functools.partial(_rglru_kernel, L=L),
        grid=(bn, nt),
        in_specs=[
            pl.BlockSpec(blk, lambda b, j: (b, 0, COL_CX // HEAD + j)),
            pl.BlockSpec(blk, lambda b, j: (b, 0, COL_CG // HEAD + j)),
            pl.BlockSpec((4, HEAD), lambda b, j: (0, j)),
            pl.BlockSpec((1, HEAD), lambda b, j: (0, j)),
            pl.BlockSpec((1, HEAD, 4 * HEAD), lambda b, j: (j, 0, 0)),
            pl.BlockSpec((1, 4, HEAD), lambda b, j: (j, 0, 0)),
            pl.BlockSpec((1, 2, HEAD), lambda b, j: (j, 0, 0)),
        ],
        out_specs=pl.BlockSpec(blk, lambda b, j: (b, 0, j)),
        out_shape=jax.ShapeDtypeStruct((bn, L, C_WIDTH), F32),
        scratch_shapes=[pltpu.VMEM((L + 2 * SUBLANES, HEAD), F32), scr, scr, scr, scr, scr],
        name="rglru",
        compiler_params=_cparams(("arbitrary", "arbitrary")),
    )(proj, proj, cw, cb, wg, bg, c8)


def _out_kernel(oa_ref, ob_ref, yc_ref, x_ref, gm_ref, rgn_ref, w_ref, npost_ref, npre_ref,
                sc_ref, sh_ref, wr_ref, x1_ref, h2_ref, pt_ref):
    a = oa_ref[...]
    b = ob_ref[...]
    c = (_rms(yc_ref[...], 1e-6) * rgn_ref[...]).astype(BF16)
    mix = jnp.dot(a, w_ref[0:A_WIDTH, :], preferred_element_type=F32)
    mix = mix + jnp.dot(b, w_ref[A_WIDTH:A_WIDTH + B_WIDTH, :], preferred_element_type=F32)
    mix = mix + jnp.dot(c, w_ref[A_WIDTH + B_WIDTH:, :], preferred_element_type=F32)
    x1 = x_ref[...] + gm_ref[0] * (_rms(mix, 1e-6) * npost_ref[...])
    x1_ref[...] = x1
    h2 = _rms(x1, 1e-6) * npre_ref[...]
    h2 = h2 * (1.0 + sc_ref[0]) + sh_ref[0]
    hb = h2.astype(BF16)
    lo = lax.bitcast_convert_type(hb[:, :D_MODEL // 2].astype(F32), U32)
    hi = lax.bitcast_convert_type(hb[:, D_MODEL // 2:].astype(F32), U32)
    h2_ref[...] = hi | lax.shift_right_logical(lo, jnp.uint32(16))
    logits = _dot_nt(wr_ref[...], h2, precision=HIGHEST)
    m = jnp.max(logits, axis=0, keepdims=True)
    e = jnp.exp(logits - m)
    pt_ref[...] = e / jnp.sum(e, axis=0, keepdims=True)


def _out_call(oa, ob, yc, x, gm, rgn, w_bf16, npost, npre, sc, sh, wr_t, L):
    M = x.shape[0]
    tm = 512 if L % 512 == 0 else L
    lt = L // tm
    row = lambda w: pl.BlockSpec((tm, w), lambda i: (i, 0))
    vec = lambda w: pl.BlockSpec((1, w), lambda i: (0, 0))
    per_b = pl.BlockSpec((1, 1, D_MODEL), lambda i: (i // lt, 0, 0))
    return pl.pallas_call(
        _out_kernel,
        grid=(M // tm,),
        in_specs=[
            row(A_WIDTH), row(B_WIDTH), row(C_WIDTH), row(D_MODEL),
            per_b, vec(C_WIDTH),
            pl.BlockSpec((D_MODEL, D_MODEL), lambda i: (0, 0), pipeline_mode=pl.Buffered(1)),
            vec(D_MODEL), vec(D_MODEL), per_b, per_b,
            pl.BlockSpec((N_EXPERTS, D_MODEL), lambda i: (0, 0)),
        ],
        out_specs=[row(D_MODEL), row(D_MODEL // 2), pl.BlockSpec((N_EXPERTS, tm), lambda i: (0, i))],
        out_shape=[
            jax.ShapeDtypeStruct((M, D_MODEL), F32),
            jax.ShapeDtypeStruct((M, D_MODEL // 2), U32),
            jax.ShapeDtypeStruct((N_EXPERTS, M), F32),
        ],
        name="out_proj_router",
        compiler_params=_cparams(("arbitrary",)),
    )(oa, ob, yc, x, gm, rgn.reshape(1, C_WIDTH), w_bf16, npost.reshape(1, D_MODEL),
      npre.reshape(1, D_MODEL), sc, sh, wr_t)


def _select_kernel(p_ref, sel_ref, *, cap, n):
    bits = lax.bitcast_convert_type(p_ref[...], I32)
    idx = lax.broadcasted_iota(I32, bits.shape, 1)

    def count(mask):
        return jnp.sum(mask.astype(F32), axis=-1, keepdims=True).astype(I32)

    def value_step(i, ans):
        cand = ans | lax.shift_left(jnp.int32(1), 30 - i)
        return jnp.where(count(bits >= cand) >= cap, cand, ans)

    thr = lax.fori_loop(0, 31, value_step, jnp.zeros((N_EXPERTS, 1), I32))
    gt = bits > thr
    eq = bits == thr
    need = cap - count(gt)

    def index_step(i, lohi):
        lo, hi = lohi
        mid = lax.shift_right_arithmetic(lo + hi, 1)
        ok = count(eq & (idx <= mid)) >= need
        return jnp.where(ok, lo, mid + 1), jnp.where(ok, mid, hi)

    steps = max(1, (n - 1).bit_length())
    lo, _ = lax.fori_loop(0, steps, index_step,
                          (jnp.zeros((N_EXPERTS, 1), I32), jnp.full((N_EXPERTS, 1), n - 1, I32)))
    sel_ref[...] = (gt | (eq & (idx <= lo))).astype(I32)


def _select_call(probs_t, cap):
    n = probs_t.shape[1]
    return pl.pallas_call(
        functools.partial(_select_kernel, cap=cap, n=n),
        out_shape=jax.ShapeDtypeStruct((N_EXPERTS, n), I32),
        name="ec_select",
        compiler_params=pltpu.CompilerParams(vmem_limit_bytes=VMEM_LIMIT),
    )(probs_t)


def _compact_kernel(m_ref, p0_ref, p1_ref, excl_ref, c0_ref, c1_ref, *, rows, fill0, fill1):
    mask = m_ref[...]
    mask_f = mask.astype(F32)
    li = lax.broadcasted_iota(I32, (LANES, LANES), 0)
    lj = lax.broadcasted_iota(I32, (LANES, LANES), 1)
    upper = (li <= lj).astype(BF16)
    c_row = jnp.dot(mask.astype(BF16), upper, preferred_element_type=F32)
    rb = min(rows, 256)
    ri = lax.broadcasted_iota(I32, (rb, rb), 0)
    rj = lax.broadcasted_iota(I32, (rb, rb), 1)
    strict = (rj < ri).astype(BF16)
    carry = jnp.zeros((1, LANES), F32)
    offs = []
    for blk in range(rows // rb):
        tot = jnp.broadcast_to(c_row[blk * rb:(blk + 1) * rb, LANES - 1:LANES], (rb, LANES))
        pre = jnp.dot(strict, tot.astype(BF16), preferred_element_type=F32) + carry
        offs.append(pre)
        carry = pre[rb - 1:rb] + tot[rb - 1:rb]
    row_off = offs[0] if len(offs) == 1 else jnp.concatenate(offs, axis=0)
    excl = (row_off + c_row - mask_f).astype(I32)
    excl_ref[...] = excl

    lane = lax.broadcasted_iota(I32, (rows, LANES), 1)
    flat = lax.broadcasted_iota(I32, (rows, LANES), 0) * LANES + lane
    valid = mask
    disp = jnp.where(mask != 0, flat - excl, 0)
    pay0 = p0_ref[...]
    pay1 = p1_ref[...]
    nbits = (rows * LANES - 1).bit_length()
    for bit in range(nbits):
        s = 1 << bit
        if s < LANES:
            def shift(x, s=s):
                t = pltpu.roll(x, LANES - s, 1)
                t2 = pltpu.roll(t, rows - 1, 0)
                return jnp.where(lane < LANES - s, t, t2)
        else:
            def shift(x, s=s):
                return pltpu.roll(x, rows - s // LANES, 0)
        moving = valid & (lax.shift_right_logical(disp, bit) & 1)
        arrive = shift(moving) != 0
        disp = jnp.where(arrive, shift(disp), disp)
        pay0 = jnp.where(arrive, shift(pay0), pay0)
        pay1 = jnp.where(arrive, shift(pay1), pay1)
        valid = jnp.where(arrive, 1, valid & (1 - moving))
    c0_ref[...] = jnp.where(valid != 0, pay0, fill0)
    c1_ref[...] = jnp.where(valid != 0, pay1, fill1)


def _compact_call(mask, pay0, pay1, fill0, fill1):
    rows = mask.shape[0]
    out = jax.ShapeDtypeStruct((rows, LANES), I32)
    return pl.pallas_call(
        functools.partial(_compact_kernel, rows=rows, fill0=fill0, fill1=fill1),
        out_shape=[out, out, out],
        name="ec_compact",
        compiler_params=pltpu.CompilerParams(vmem_limit_bytes=VMEM_LIMIT),
    )(mask, pay0, pay1)


def _ffn_kernel(idx_ref, idx_next_ref, h_hbm, gate_ref, wg_ref, wu_ref, wd_ref, o_ref, buf, sem,
                *, tm, nsteps, nblk):
    step = pl.program_id(0) * nblk + pl.program_id(1)
    slot = lax.rem(step, 2)

    def issue(ref, dst_slot):
        def body(r, carry):
            tok = ref[0, 0, r]
            pltpu.make_async_copy(h_hbm.at[pl.ds(tok, 1), :], buf.at[dst_slot, pl.ds(r, 1), :],
                                  sem.at[dst_slot]).start()
            return carry
        lax.fori_loop(0, tm, body, 0, unroll=8)

    @pl.when(step == 0)
    def _():
        issue(idx_ref, 0)

    @pl.when(step + 1 < nsteps)
    def _():
        issue(idx_next_ref, 1 - slot)

    pltpu.make_async_copy(h_hbm.at[pl.ds(0, tm), :], buf.at[slot], sem.at[slot]).wait()
    u = buf[slot]
    x_lo = lax.bitcast_convert_type(lax.shift_left(u, jnp.uint32(16)), F32).astype(BF16)
    x_hi = lax.bitcast_convert_type(u & jnp.uint32(0xFFFF0000), F32).astype(BF16)
    x = jnp.concatenate([x_lo, x_hi], axis=1)
    hg = jnp.dot(x, wg_ref[0], preferred_element_type=F32)
    hu = jnp.dot(x, wu_ref[0], preferred_element_type=F32)
    hid = (hg * _sigmoid(hg) * hu).astype(BF16)
    y = jnp.dot(hid, wd_ref[0], preferred_element_type=F32)
    o_ref[...] = y * gate_ref[...]


def _ffn_call(idx, h2, gates, wg, wu, wd, tm):
    n_e, slots = idx.shape
    nblk = slots // tm
    nsteps = n_e * nblk
    idx3 = idx.reshape(nsteps, 1, tm)

    def nxt(e, j):
        lin = jnp.minimum(e * nblk + j + 1, nsteps - 1)
        return (lin, 0, 0)

    return pl.pallas_call(
        functools.partial(_ffn_kernel, tm=tm, nsteps=nsteps, nblk=nblk),
        grid=(n_e, nblk),
        in_specs=[
            pl.BlockSpec((1, 1, tm), lambda e, j: (e * nblk + j, 0, 0), memory_space=pltpu.SMEM),
            pl.BlockSpec((1, 1, tm), nxt, memory_space=pltpu.SMEM),
            pl.BlockSpec(memory_space=pl.ANY),
            pl.BlockSpec((tm, 1), lambda e, j: (e * nblk + j, 0)),
            pl.BlockSpec((1, D_MODEL, D_EXPERT), lambda e, j: (e, 0, 0), pipeline_mode=pl.Buffered(1)),
            pl.BlockSpec((1, D_MODEL, D_EXPERT), lambda e, j: (e, 0, 0), pipeline_mode=pl.Buffered(1)),
            pl.BlockSpec((1, D_EXPERT, D_MODEL), lambda e, j: (e, 0, 0), pipeline_mode=pl.Buffered(1)),
        ],
        out_specs=pl.BlockSpec((tm, D_MODEL), lambda e, j: (e * nblk + j, 0)),
        out_shape=jax.ShapeDtypeStruct((n_e * slots, D_MODEL), F32),
        scratch_shapes=[pltpu.VMEM((2, tm, D_MODEL // 2), U32), pltpu.SemaphoreType.DMA((2,))],
        name="ec_ffn",
        compiler_params=_cparams(("arbitrary", "arbitrary")),
    )(idx3, idx3, h2, gates, wg, wu, wd)


CMB_VALID, CMB_FIRST, CMB_LAST, CMB_NEWWIN, CMB_PREFETCH = 1, 2, 4, 8, 16


def _combine_schedule(off, nwin, win):
    ntiles = off.shape[0] - 1
    lo = jnp.minimum(off[:-1] // win, nwin - 1)
    hi = jnp.maximum(lo, jnp.minimum((off[1:] - 1) // win, nwin - 1))
    cnt = hi - lo + 1
    start = jnp.cumsum(cnt) - cnt
    total = start[-1] + cnt[-1]
    k = jnp.arange(ntiles + nwin, dtype=I32)
    valid = k < total
    t = jnp.clip(jnp.searchsorted(start, k, side='right').astype(I32) - 1, 0, ntiles - 1)
    t = jnp.where(valid, t, ntiles - 1)
    w = jnp.where(valid, lo[t] + (k - start[t]), hi[-1])
    first = valid & (k == start[t])
    last = valid & (k == start[t] + cnt[t] - 1)
    neww = valid & (w != jnp.concatenate([jnp.full((1,), -1, I32), w[:-1]]))
    prefetch = neww & (w + 1 <= hi[-1])
    flags = (valid * CMB_VALID + first * CMB_FIRST + last * CMB_LAST + neww * CMB_NEWWIN
             + prefetch * CMB_PREFETCH).astype(I32)
    return t, w.astype(I32), flags


def _combine_kernel(tile_ref, win_ref, flag_ref, src_ref, src_next_ref, tok_ref, ye_hbm, x1_ref,
                    gf_ref, nw_ref, o_ref, zbuf, z_hi, z_lo, acc, sem, *, tt, win):
    k = pl.program_id(0)
    flags = flag_ref[k]
    w = win_ref[k]
    slot = lax.rem(w, 2)

    def issue(ref, dst_slot):
        def row(r, carry):
            s = ref[0, 0, r]
            pltpu.make_async_copy(ye_hbm.at[pl.ds(s, 1), :], zbuf.at[dst_slot, pl.ds(r, 1), :],
                                  sem.at[dst_slot]).start()
            return carry
        lax.fori_loop(0, win, row, 0, unroll=8)

    @pl.when(k == 0)
    def _():
        issue(src_ref, slot)

    @pl.when((flags & CMB_NEWWIN) != 0)
    def _():
        pltpu.make_async_copy(ye_hbm.at[pl.ds(0, win), :], zbuf.at[slot], sem.at[slot]).wait()

        @pl.when((flags & CMB_PREFETCH) != 0)
        def _():
            issue(src_next_ref, 1 - slot)

        z = zbuf[slot]
        hi = z.astype(BF16)
        z_hi[...] = hi
        z_lo[...] = (z - hi.astype(F32)).astype(BF16)

    @pl.when((flags & CMB_FIRST) != 0)
    def _():
        acc[...] = jnp.zeros_like(acc)

    @pl.when((flags & CMB_VALID) != 0)
    def _():
        tok_col = tile_ref[k] * tt + lax.broadcasted_iota(I32, (tt, 1), 0)
        seg = jnp.concatenate([(tok_ref[0, kk:kk + 1, :] == tok_col).astype(BF16)
                               for kk in range(win // LANES)], axis=1)
        acc[...] = (acc[...] + jnp.dot(seg, z_hi[...], preferred_element_type=F32)
                    + jnp.dot(seg, z_lo[...], preferred_element_type=F32))

    @pl.when((flags & CMB_LAST) != 0)
    def _():
        o_ref[...] = x1_ref[...] + gf_ref[0] * (_rms(acc[...], 1e-6) * nw_ref[...])


def _combine_call(off, src, tok, ye, x1, gf, nw, L, tt, win):
    M = x1.shape[0]
    lt = L // tt
    nwin = src.shape[0] // win
    tile_k, win_k, flag_k = _combine_schedule(off, nwin, win)
    nxt = lambda k, t, w, f: (jnp.minimum(w[k] + 1, nwin - 1), 0, 0)
    grid_spec = pltpu.PrefetchScalarGridSpec(
        num_scalar_prefetch=3,
        grid=(tile_k.shape[0],),
        in_specs=[
            pl.BlockSpec((1, 1, win), lambda k, t, w, f: (w[k], 0, 0), memory_space=pltpu.SMEM),
            pl.BlockSpec((1, 1, win), nxt, memory_space=pltpu.SMEM),
            pl.BlockSpec((1, win // LANES, LANES), lambda k, t, w, f: (w[k], 0, 0)),
            pl.BlockSpec(memory_space=pl.ANY),
            pl.BlockSpec((tt, D_MODEL), lambda k, t, w, f: (t[k], 0)),
            pl.BlockSpec((1, 1, D_MODEL), lambda k, t, w, f: (t[k] // lt, 0, 0)),
            pl.BlockSpec((1, D_MODEL), lambda k, t, w, f: (0, 0)),
        ],
        out_specs=pl.BlockSpec((tt, D_MODEL), lambda k, t, w, f: (t[k], 0)),
        scratch_shapes=[
            pltpu.VMEM((2, win, D_MODEL), F32),
            pltpu.VMEM((win, D_MODEL), BF16),
            pltpu.VMEM((win, D_MODEL), BF16),
            pltpu.VMEM((tt, D_MODEL), F32),
            pltpu.SemaphoreType.DMA((2,)),
        ],
    )
    src3 = src.reshape(nwin, 1, win)
    return pl.pallas_call(
        functools.partial(_combine_kernel, tt=tt, win=win),
        grid_spec=grid_spec,
        out_shape=jax.ShapeDtypeStruct((M, D_MODEL), F32),
        name="ec_combine",
        compiler_params=_cparams(("arbitrary",)),
    )(tile_k, win_k, flag_k, src3, src3, tok.reshape(nwin, win // LANES, LANES), ye, x1, gf,
      nw.reshape(1, D_MODEL))


def _route_group(probs_t, tok_base, slot_base, slots_total, tt):
    n = probs_t.shape[1]
    cap = max(1, EC_FACTOR * n // N_EXPERTS)
    rows = N_EXPERTS * n // LANES
    sel = _select_call(probs_t, cap)
    tok_ids = tok_base + lax.broadcasted_iota(I32, (N_EXPERTS, n), 1)
    excl_e, idx_c, gate_c = _compact_call(
        sel.reshape(rows, LANES), tok_ids.reshape(rows, LANES),
        lax.bitcast_convert_type(probs_t, I32).reshape(rows, LANES), tok_base, 0)
    npair = N_EXPERTS * cap
    idx_e = idx_c.reshape(-1)[:npair].reshape(N_EXPERTS, cap)
    gate_e = lax.bitcast_convert_type(gate_c.reshape(-1)[:npair], F32).reshape(N_EXPERTS, cap)
    e_col = lax.broadcasted_iota(I32, (N_EXPERTS, n), 0)
    src = excl_e.reshape(N_EXPERTS, n) - e_col * cap + e_col * slots_total + slot_base
    excl_t, src_c, tok_c = _compact_call(
        sel.T.reshape(rows, LANES), src.T.reshape(rows, LANES),
        tok_ids.T.reshape(rows, LANES), 0, -1)
    src_t = src_c.reshape(-1)[:npair]
    tok_t = tok_c.reshape(-1)[:npair]
    off = excl_t.reshape(-1)[::tt * N_EXPERTS]
    return idx_e, gate_e, src_t, tok_t, off, npair


def _block_diag_tiles(w):
    nt = C_WIDTH // HEAD
    per = HEAD // C_BLOCK
    w = w.reshape(2, nt, per, C_BLOCK, C_BLOCK)
    eye = jnp.eye(per, dtype=w.dtype)
    t = jnp.einsum('dtpce,pq->dtpcqe', w, eye)
    return t.reshape(2, nt, HEAD, HEAD).transpose(1, 0, 2, 3)


def kernel(x_prompt, x_sample, c_prompt, c_sample, ada_w, ada_b, norm_mix_pre, norm_mix_post, norm_ffn_pre, norm_ffn_post, w_in, hg_lower, hg_norm, dl_q1, dl_k1, dl_q2, dl_k2, dl_subln, conv_w, conv_b, rg_wa, rg_ba, rg_wx, rg_bx, rg_lambda, rg_norm, w_out, w_router, w_gate, w_up, w_down):
    bp, L, D = x_prompt.shape
    bs = x_sample.shape[0]
    bn = bp + bs
    n_p, n_s = bp * L, bs * x_sample.shape[1]
    M = n_p + n_s
    x = jnp.concatenate([x_prompt, x_sample], axis=0)
    c = jnp.concatenate([c_prompt, c_sample], axis=0)
    mod = _ada_call(c, ada_w, ada_b)

    lb_soft = jax.nn.softmax(hg_lower.astype(F32), axis=0)
    lb_all = jnp.cumsum(lb_soft, axis=0) - lb_soft[0:1]
    half = ROT_DIM // 2
    inv_freq = ROPE_THETA ** (-jnp.arange(half, dtype=F32) / half)
    ang = jnp.arange(L, dtype=F32)[:, None] * inv_freq[None, :]
    cos, sin = jnp.cos(ang), jnp.sin(ang)
    one = jnp.ones((L, B_DQK - ROT_DIM), F32)
    zero = jnp.zeros((L, B_DQK - ROT_DIM), F32)
    zh = jnp.zeros((L, half), F32)
    cos_t = jnp.tile(jnp.concatenate([cos, cos, one], axis=1), (1, 2))
    s1_t = jnp.tile(jnp.concatenate([-sin, zh, zero], axis=1), (1, 2))
    s2_t = jnp.tile(jnp.concatenate([zh, sin, zero], axis=1), (1, 2))

    cap_p = max(1, EC_FACTOR * n_p // N_EXPERTS)
    cap_s = max(1, EC_FACTOR * n_s // N_EXPERTS)
    slots_total = cap_p + cap_s
    tm_ffn = math.gcd(512, math.gcd(cap_p, cap_s))
    tt = 256 if L % 256 == 0 else L
    win = 256

    xf = x
    for l in range(DEPTH):
        m6 = mod[l].reshape(bn, N_MOD, 1, D)
        sh_m, sc_m, g_m, sh_f, sc_f, g_f = (m6[:, i] for i in range(N_MOD))

        proj = _in_call(xf, norm_mix_pre[l], sc_m, sh_m, w_in[l].astype(BF16))

        lb = lb_all[l].reshape(A_HEADS, HEAD)
        par = jnp.stack([jnp.maximum(lb, LB_MIN), 1.0 - lb], axis=1)
        o_a = _hgrn_call(proj, par, hg_norm[l])

        lam_init = 0.8 - 0.6 * math.exp(-0.3 * l)
        lam = (jnp.exp(jnp.sum(dl_q1[l].astype(F32) * dl_k1[l].astype(F32)))
               - jnp.exp(jnp.sum(dl_q2[l].astype(F32) * dl_k2[l].astype(F32))) + lam_init)
        lam2 = jnp.stack([lam, jnp.asarray(1.0 - lam_init, F32)])
        qb, kb, vb = _qkv_call(proj, cos_t, s1_t, s2_t)
        o_b = _attn_call(lam2, qb, kb, vb, dl_subln[l])

        nt = C_WIDTH // HEAD
        wa_t = _block_diag_tiles(rg_wa[l])
        wx_t = _block_diag_tiles(rg_wx[l])
        wg = jnp.concatenate([wa_t[:, 0], wx_t[:, 0], wa_t[:, 1], wx_t[:, 1]], axis=-1).astype(BF16)
        bg = jnp.stack([rg_ba[l, 0], rg_bx[l, 0], rg_ba[l, 1], rg_bx[l, 1]], axis=0)
        bg = bg.reshape(4, nt, HEAD).transpose(1, 0, 2)
        c8 = (-RG_C * jax.nn.softplus(-rg_lambda[l])).reshape(2, nt, HEAD).transpose(1, 0, 2)
        y_c = _rglru_call(proj, conv_w[l], conv_b[l].reshape(1, C_WIDTH), wg, bg, c8)

        x1, h2, probs_t = _out_call(
            o_a.reshape(M, A_WIDTH), o_b.reshape(M, B_WIDTH), y_c.reshape(M, C_WIDTH),
            xf.reshape(M, D), g_m, rg_norm[l], w_out[l].astype(BF16), norm_mix_post[l],
            norm_ffn_pre[l], sc_f, sh_f, w_router[l].T, L)

        ie_p, ge_p, src_p, tok_p, off_p, np_p = _route_group(probs_t[:, :n_p], 0, 0, slots_total, tt)
        ie_s, ge_s, src_s, tok_s, off_s, np_s = _route_group(probs_t[:, n_p:], n_p, cap_p, slots_total, tt)
        idx = jnp.concatenate([ie_p, ie_s], axis=1)
        gates = jnp.concatenate([ge_p, ge_s], axis=1).reshape(N_EXPERTS * slots_total, 1)
        ye = _ffn_call(idx, h2, gates, w_gate[l].astype(BF16), w_up[l].astype(BF16),
                       w_down[l].astype(BF16), tm_ffn)

        pad = -(np_p + np_s) % win
        src = jnp.concatenate([src_p, src_s, jnp.zeros((pad,), I32)])
        tok = jnp.concatenate([tok_p, tok_s, jnp.full((pad,), -1, I32)])
        off = jnp.concatenate([off_p, off_s + np_p, jnp.full((1,), np_p + np_s, I32)])
        x2 = _combine_call(off, src, tok, ye, x1, g_f, norm_ffn_post[l], L, tt, win)
        xf = x2.reshape(bn, L, D)

    return xf[:bp], xf[bp:]
```

```python
import functools
import math

import jax
import jax.numpy as jnp
from jax import lax
from jax.experimental import pallas as pl
from jax.experimental.pallas import tpu as pltpu

F32 = jnp.float32
BF16 = jnp.bfloat16
I32 = jnp.int32
U32 = jnp.uint32
HIGHEST = lax.Precision.HIGHEST

D_MODEL = 2048
DEPTH = 2
A_WIDTH = D_MODEL // 4
B_WIDTH = D_MODEL // 2
C_WIDTH = D_MODEL // 4
HEAD = 128
A_HEADS = A_WIDTH // HEAD
B_HEADS = B_WIDTH // HEAD
B_DQK = HEAD // 2
ROT_DIM = B_DQK // 4
ROPE_THETA = 500000.0
LB_MIN = 1e-12
C_BLOCKS = 8
C_BLOCK = C_WIDTH // C_BLOCKS
RG_C = 8.0
N_EXPERTS = 16
EC_FACTOR = 2
D_EXPERT = D_MODEL // 2
N_MOD = 6
IN_COLS = 3 * A_WIDTH + 2 * A_WIDTH + 3 * B_WIDTH + 2 * C_WIDTH
COL_AQ, COL_AFF, COL_AFB, COL_AI, COL_AG = 0, 512, 1024, 1536, 2048
COL_BQ, COL_BK, COL_BV = 2560, 3584, 4608
COL_CX, COL_CG = 5632, 6144

LANES = 128
SUBLANES = 8
VMEM_LIMIT = 56 * 1024 * 1024

HGRN_CHUNK = 64
HGRN_SUB = 16
HGRN_BLOCK = 256
HGRN_SAFE_RANGE = 80.0
ATTN_TQ = 1024
ATTN_SUB = 256


def _cparams(sem):
    return pltpu.CompilerParams(dimension_semantics=sem, vmem_limit_bytes=VMEM_LIMIT)


def _sigmoid(x):
    return 1.0 / (1.0 + jnp.exp(-x))


def _rms(x, eps):
    return x * lax.rsqrt(jnp.mean(x * x, axis=-1, keepdims=True) + eps)


def _dot_nt(a, b, **kw):
    return lax.dot_general(a, b, (((1,), (1,)), ((), ())), preferred_element_type=F32, **kw)


def _dot_tn(a, b):
    return lax.dot_general(a, b, (((0,), (0,)), ((), ())), preferred_element_type=F32)


def _ada_kernel(c_ref, w_ref, b_ref, o_ref):
    c = c_ref[...]
    a = (c * _sigmoid(c)).astype(BF16)
    o_ref[0] = jnp.dot(a, w_ref[0].astype(BF16), preferred_element_type=F32) + b_ref[0]


def _ada_call(c, ada_w, ada_b):
    bn = c.shape[0]
    tn = 1024
    ncol = N_MOD * D_MODEL
    return pl.pallas_call(
        _ada_kernel,
        grid=(DEPTH, ncol // tn),
        in_specs=[
            pl.BlockSpec((bn, D_MODEL), lambda l, j: (0, 0)),
            pl.BlockSpec((1, D_MODEL, tn), lambda l, j: (l, 0, j)),
            pl.BlockSpec((1, 1, tn), lambda l, j: (l, 0, j)),
        ],
        out_specs=pl.BlockSpec((1, bn, tn), lambda l, j: (l, 0, j)),
        out_shape=jax.ShapeDtypeStruct((DEPTH, bn, ncol), F32),
        name="ada_mod",
        compiler_params=_cparams(("arbitrary", "arbitrary")),
    )(c, ada_w, ada_b.reshape(DEPTH, 1, ncol))


def _in_kernel(x_ref, nw_ref, sc_ref, sh_ref, w_ref, o_ref, h_scr):
    @pl.when(pl.program_id(1) == 0)
    def _():
        h = _rms(x_ref[0], 1e-6) * nw_ref[...]
        h = h * (1.0 + sc_ref[0]) + sh_ref[0]
        h_scr[...] = h.astype(BF16)

    o_ref[0] = jnp.dot(h_scr[...], w_ref[...], preferred_element_type=F32)


def _in_call(x, nw, sc, sh, w_bf16):
    bn, L, _ = x.shape
    tm = min(1024, L)
    tn = 512
    lt = L // tm
    return pl.pallas_call(
        _in_kernel,
        grid=(bn * lt, IN_COLS // tn),
        in_specs=[
            pl.BlockSpec((1, tm, D_MODEL), lambda i, j: (i // lt, i % lt, 0)),
            pl.BlockSpec((1, D_MODEL), lambda i, j: (0, 0)),
            pl.BlockSpec((1, 1, D_MODEL), lambda i, j: (i // lt, 0, 0)),
            pl.BlockSpec((1, 1, D_MODEL), lambda i, j: (i // lt, 0, 0)),
            pl.BlockSpec((D_MODEL, tn), lambda i, j: (0, j)),
        ],
        out_specs=pl.BlockSpec((1, tm, tn), lambda i, j: (i // lt, i % lt, j)),
        out_shape=jax.ShapeDtypeStruct((bn, L, IN_COLS), F32),
        scratch_shapes=[pltpu.VMEM((tm, D_MODEL), BF16)],
        name="in_proj",
        compiler_params=_cparams(("arbitrary", "arbitrary")),
    )(x, nw.reshape(1, D_MODEL), sc, sh, w_bf16)


def _sigmoid_pair(z):
    e = jnp.exp(-jnp.abs(z))
    r = 1.0 / (1.0 + e)
    er = e * r
    pos = z >= 0.0
    return jnp.where(pos, r, er), jnp.where(pos, er, r)


def _split3(x):
    hi = x.astype(BF16)
    r = x - hi.astype(F32)
    mid = r.astype(BF16)
    lo = (r - mid.astype(F32)).astype(BF16)
    return hi, mid, lo


def _hgrn_kernel(q_ref, ff_ref, fb_ref, i_ref, g_ref, par_ref, nw_ref, o_ref,
                 qs_scr, b_scr, k_scr, acc_scr, qe_scr, u_scr, dec_scr, st_scr, *, L):
    C, SB = HGRN_CHUNK, HGRN_SUB
    BLK = min(HGRN_BLOCK, L)
    cpb = BLK // C
    nblk = L // BLK
    nchunk = L // C
    nsb = C // SB
    lb_floor = par_ref[0, 0:1, :]
    one_m_lb = par_ref[0, 1:2, :]
    brow = lax.broadcasted_iota(I32, (BLK, BLK), 0)
    bcol = lax.broadcasted_iota(I32, (BLK, BLK), 1)
    same_chunk = (brow // C) == (bcol // C)
    sub_row = lax.broadcasted_iota(I32, (SB, 1), 0)

    ql = q_ref[0]
    qs_scr[...] = ql * _sigmoid_pair(ql)[0]

    def run_dir(f_ref, rev, first):
        causal = same_chunk & ((bcol >= brow) if rev else (bcol <= brow))
        tri = causal.astype(BF16)
        mid_off = C // 2
        end_off = 0 if rev else C - 1
        beg_off = C - 1 if rev else 0

        def gates(bi, rng):
            r0 = pl.multiple_of(bi * BLK, BLK)
            z = f_ref[0, pl.ds(r0, BLK), :]
            sig, sig_neg = _sigmoid_pair(z)
            log_f = jnp.log(lb_floor + one_m_lb * sig)
            k_scr[pl.ds(r0, BLK), :] = one_m_lb * sig_neg
            hi, mid, lo = _split3(log_f)
            b = (jnp.dot(tri, hi, preferred_element_type=F32)
                 + jnp.dot(tri, mid, preferred_element_type=F32)
                 + jnp.dot(tri, lo, preferred_element_type=F32))
            b_scr[pl.ds(r0, BLK), :] = b
            for c in range(cpb):
                m = b[c * C + mid_off:c * C + mid_off + 1]
                rng = jnp.maximum(rng, b[c * C + beg_off:c * C + beg_off + 1] - m)
                rng = jnp.maximum(rng, m - b[c * C + end_off:c * C + end_off + 1])
            return rng

        rng = lax.fori_loop(0, nblk, gates, jnp.zeros((1, HEAD), F32), unroll=2)
        safe = jnp.max(rng) <= HGRN_SAFE_RANGE

        @pl.when(safe)
        def _():
            def intra(bi, carry):
                r0 = pl.multiple_of(bi * BLK, BLK)
                b = b_scr[pl.ds(r0, BLK), :]
                k = k_scr[pl.ds(r0, BLK), :]
                q = qs_scr[pl.ds(r0, BLK), :]
                v_bf = i_ref[0, pl.ds(r0, BLK), :].astype(BF16)
                m = jnp.concatenate(
                    [jnp.broadcast_to(b[c * C + mid_off:c * C + mid_off + 1], (C, HEAD)) for c in range(cpb)], axis=0)
                b_end = jnp.concatenate(
                    [jnp.broadcast_to(b[c * C + end_off:c * C + end_off + 1], (C, HEAD)) for c in range(cpb)], axis=0)
                qt = (q * jnp.exp(b - m)).astype(BF16)
                kt = (k * jnp.exp(m - b)).astype(BF16)
                sc = _dot_nt(qt, kt)
                sc = jnp.where(causal, sc, 0.0).astype(BF16)
                o_intra = jnp.dot(sc, v_bf, preferred_element_type=F32)
                if first:
                    acc_scr[pl.ds(r0, BLK), :] = o_intra
                else:
                    acc_scr[pl.ds(r0, BLK), :] = acc_scr[pl.ds(r0, BLK), :] + o_intra
                qe_scr[pl.ds(r0, BLK), :] = (q * jnp.exp(b)).astype(BF16)
                kend = (k * jnp.exp(b_end - b)).astype(BF16)
                for c in range(cpb):
                    rows = slice(c * C, (c + 1) * C)
                    u_scr[bi * cpb + c] = _dot_tn(v_bf[rows], kend[rows])
                    dec_scr[pl.ds(bi * cpb + c, 1), :] = jnp.exp(b[c * C + end_off:c * C + end_off + 1])
                return carry

            lax.fori_loop(0, nblk, intra, 0, unroll=2)

            def inter(ci, st):
                c = (nchunk - 1 - ci) if rev else ci
                r0 = pl.multiple_of(c * C, C)
                o_state = _dot_nt(qe_scr[pl.ds(r0, C), :], st.astype(BF16))
                acc_scr[pl.ds(r0, C), :] = acc_scr[pl.ds(r0, C), :] + o_state
                return st * dec_scr[pl.ds(c, 1), :] + u_scr[c]

            lax.fori_loop(0, nchunk, inter, jnp.zeros((HEAD, HEAD), F32), unroll=4)

        @pl.when(jnp.logical_not(safe))
        def _():
            order = list(range(nsb - 1, -1, -1)) if rev else list(range(nsb))
            st_scr[...] = jnp.zeros_like(st_scr)

            def body(ci, carry):
                c = (nchunk - 1 - ci) if rev else ci
                r0 = pl.multiple_of(c * C, C)
                b = b_scr[pl.ds(r0, C), :]
                k = k_scr[pl.ds(r0, C), :]
                q = qs_scr[pl.ds(r0, C), :]
                v = i_ref[0, pl.ds(r0, C), :]
                st = st_scr[...]
                o_state = _dot_nt((q * jnp.exp(b)).astype(BF16), st.astype(BF16))
                v_bf = v.astype(BF16)
                for p, blk in enumerate(order):
                    lo = SB * blk
                    b_blk = b[lo:lo + SB]
                    q_blk = q[lo:lo + SB]
                    k_blk = k[lo:lo + SB]
                    v_blk = v[lo:lo + SB]
                    out = o_state[lo:lo + SB]
                    if p > 0:
                        if rev:
                            bound = b[lo + SB:lo + SB + 1]
                            e0, e1 = lo + SB, C
                        else:
                            bound = b[lo - 1:lo]
                            e0, e1 = 0, lo
                        qt = (q_blk * jnp.exp(b_blk - bound)).astype(BF16)
                        kt = (k[e0:e1] * jnp.exp(bound - b[e0:e1])).astype(BF16)
                        sc = _dot_nt(qt, kt)
                        out = out + jnp.dot(sc.astype(BF16), v_bf[e0:e1], preferred_element_type=F32)
                    diag = jnp.zeros((SB, HEAD), F32)
                    for t in range(SB):
                        bt = b_blk[t:t + 1]
                        pm = k_blk * jnp.exp(jnp.minimum(bt - b_blk, 0.0)) * q_blk[t:t + 1]
                        s = jnp.sum(pm, axis=-1, keepdims=True)
                        keep = (sub_row >= t) if rev else (sub_row <= t)
                        s = jnp.where(keep, s, 0.0)
                        o_t = jnp.sum(s * v_blk, axis=0, keepdims=True)
                        diag = jnp.where(sub_row == t, o_t, diag)
                    out = out + diag
                    rows = pl.ds(r0 + lo, SB)
                    if first:
                        acc_scr[rows, :] = out
                    else:
                        acc_scr[rows, :] = acc_scr[rows, :] + out
                b_end = b[0:1] if rev else b[C - 1:C]
                kend = (k * jnp.exp(b_end - b)).astype(BF16)
                st_scr[...] = st * jnp.exp(b_end) + _dot_tn(v_bf, kend)
                return carry

            lax.fori_loop(0, nchunk, body, 0)

    run_dir(ff_ref, False, True)
    run_dir(fb_ref, True, False)
    g = g_ref[0]
    o_ref[0] = (_rms(acc_scr[...], 1e-6) * nw_ref[...] * (g * _sigmoid(g))).astype(o_ref.dtype)


def _hgrn_call(proj, par, nw):
    bn, L, _ = proj.shape
    cb = lambda off: (lambda b, h: (b, 0, off // HEAD + h))
    blk = (1, L, HEAD)
    nchunk = L // HGRN_CHUNK
    seq = pltpu.VMEM((L, HEAD), F32)
    return pl.pallas_call(
        functools.partial(_hgrn_kernel, L=L),
        grid=(bn, A_HEADS),
        in_specs=[
            pl.BlockSpec(blk, cb(COL_AQ)),
            pl.BlockSpec(blk, cb(COL_AFF)),
            pl.BlockSpec(blk, cb(COL_AFB)),
            pl.BlockSpec(blk, cb(COL_AI)),
            pl.BlockSpec(blk, cb(COL_AG)),
            pl.BlockSpec((1, 2, HEAD), lambda b, h: (h, 0, 0)),
            pl.BlockSpec((1, HEAD), lambda b, h: (0, 0)),
        ],
        out_specs=pl.BlockSpec(blk, lambda b, h: (b, 0, h)),
        out_shape=jax.ShapeDtypeStruct((bn, L, A_WIDTH), BF16),
        scratch_shapes=[
            seq, seq, seq, seq,
            pltpu.VMEM((L, HEAD), BF16),
            pltpu.VMEM((nchunk, HEAD, HEAD), F32),
            pltpu.VMEM((nchunk, HEAD), F32),
            pltpu.VMEM((HEAD, HEAD), F32),
        ],
        name="hgrn2",
        compiler_params=_cparams(("arbitrary", "arbitrary")),
    )(proj, proj, proj, proj, proj, par, nw.reshape(1, HEAD))


def _qkv_kernel(q_ref, k_ref, v_ref, cos_ref, s1_ref, s2_ref, qo_ref, ko_ref, vo_ref):
    cos = cos_ref[...]
    s1 = s1_ref[...]
    s2 = s2_ref[...]
    scale = B_DQK ** -0.5 * math.log2(math.e)
    for h in range(q_ref.shape[2] // HEAD):
        sl = slice(HEAD * h, HEAD * (h + 1))
        for src, dst, mul in ((q_ref, qo_ref, scale), (k_ref, ko_ref, 1.0)):
            x = src[0, :, sl]
            xr = x * cos + pltpu.roll(x, HEAD - ROT_DIM // 2, 1) * s1 + pltpu.roll(x, ROT_DIM // 2, 1) * s2
            dst[0, :, sl] = (xr * mul).astype(BF16)
    vo_ref[0] = v_ref[0].astype(BF16)


def _qkv_call(proj, cos_t, s1_t, s2_t):
    bn, L, _ = proj.shape
    tl = min(512, L)
    wb = 512
    nj = B_WIDTH // wb
    cb = lambda off: (lambda b, i, j: (b, i, off // wb + j))
    out = jax.ShapeDtypeStruct((bn, L, B_WIDTH), BF16)
    ospec = pl.BlockSpec((1, tl, wb), lambda b, i, j: (b, i, j))
    tspec = pl.BlockSpec((tl, HEAD), lambda b, i, j: (i, 0))
    return pl.pallas_call(
        _qkv_kernel,
        grid=(bn, L // tl, nj),
        in_specs=[
            pl.BlockSpec((1, tl, wb), cb(COL_BQ)),
            pl.BlockSpec((1, tl, wb), cb(COL_BK)),
            pl.BlockSpec((1, tl, wb), cb(COL_BV)),
            tspec, tspec, tspec,
        ],
        out_specs=[ospec, ospec, ospec],
        out_shape=[out, out, out],
        name="rope_qkv",
        compiler_params=_cparams(("arbitrary", "arbitrary", "arbitrary")),
    )(proj, proj, proj, cos_t, s1_t, s2_t)


def _attn_kernel(lam_ref, q_ref, k_ref, v_ref, sw_ref, o_ref, vx_scr):
    @pl.when(pl.program_id(2) == 0)
    def _():
        lane = lax.broadcasted_iota(I32, (vx_scr.shape[0], HEAD), 1)
        vx_scr[:, 0:HEAD] = v_ref[0]
        vx_scr[:, HEAD:2 * HEAD] = jnp.where(lane == 0, 1.0, 0.0).astype(BF16)

    lam = lam_ref[0]
    post = lam_ref[1]
    k = k_ref[0]
    lane = lax.broadcasted_iota(I32, (1, HEAD), 1)
    sub = min(ATTN_SUB, q_ref.shape[1])
    nsub = q_ref.shape[1] // sub

    def scores(j):
        q = q_ref[0, j * sub:(j + 1) * sub, :]
        zero = jnp.zeros_like(q)
        return (_dot_nt(jnp.where(lane < B_DQK, q, zero), k),
                _dot_nt(jnp.where(lane >= B_DQK, q, zero), k))

    def weighted_values(s):
        m = jnp.max(s, axis=-1, keepdims=True)
        e = jnp.exp2((s - m).astype(BF16))
        ox = jnp.dot(e, vx_scr[...], preferred_element_type=F32)
        return ox[:, 0:HEAD], ox[:, HEAD:HEAD + 1]

    s_next = scores(0)
    for j in range(nsub):
        s1, s2 = s_next
        if j + 1 < nsub:
            s_next = scores(j + 1)
        o1, l1 = weighted_values(s1)
        o2, l2 = weighted_values(s2)
        o = o1 * (1.0 / l1) - o2 * (lam / l2)
        o_ref[0, j * sub:(j + 1) * sub, :] = (_rms(o, 1e-5) * sw_ref[...] * post).astype(o_ref.dtype)


def _attn_call(lam2, qb, kb, vb, sw):
    bn, L, _ = qb.shape
    tq = min(ATTN_TQ, L)
    return pl.pallas_call(
        _attn_kernel,
        grid=(bn, B_HEADS, L // tq),
        in_specs=[
            pl.BlockSpec(memory_space=pltpu.SMEM),
            pl.BlockSpec((1, tq, HEAD), lambda b, h, i: (b, i, h)),
            pl.BlockSpec((1, L, HEAD), lambda b, h, i: (b, 0, h)),
            pl.BlockSpec((1, L, HEAD), lambda b, h, i: (b, 0, h)),
            pl.BlockSpec((1, HEAD), lambda b, h, i: (0, 0)),
        ],
        out_specs=pl.BlockSpec((1, tq, HEAD), lambda b, h, i: (b, i, h)),
        out_shape=jax.ShapeDtypeStruct((bn, L, B_WIDTH), BF16),
        scratch_shapes=[pltpu.VMEM((L, 2 * HEAD), BF16)],
        name="diff_attn",
        compiler_params=_cparams(("arbitrary", "arbitrary", "arbitrary")),
    )(lam2, qb, kb, vb, sw.reshape(1, HEAD))


def _rglru_kernel(x_ref, g_ref, cw_ref, cb_ref, wg_ref, bg_ref, c8_ref, o_ref,
                  xs, a_f, x_f, a_b, x_b, *, L):
    pad = SUBLANES
    xs[0:pad, :] = jnp.zeros((pad, HEAD), F32)
    xs[pad + L:2 * pad + L, :] = jnp.zeros((pad, HEAD), F32)
    xs[pad:pad + L, :] = x_ref[0]
    cw = cw_ref[...]
    cb = cb_ref[...]
    wg = wg_ref[0]
    bg = bg_ref[0]
    c8 = c8_ref[0]
    tc = min(256, L)
    for ci in range(L // tc):
        r0 = ci * tc
        u = cb
        for j in range(4):
            u = u + xs[pad - 2 + j + r0:pad - 2 + j + r0 + tc, :] * cw[j:j + 1]
        gates = jnp.dot(u.astype(BF16), wg, preferred_element_type=F32)
        for d, (a_scr, x_scr) in enumerate(((a_f, x_f), (a_b, x_b))):
            r = _sigmoid(gates[:, (2 * d) * HEAD:(2 * d + 1) * HEAD] + bg[2 * d:2 * d + 1])
            ig = _sigmoid(gates[:, (2 * d + 1) * HEAD:(2 * d + 2) * HEAD] + bg[2 * d + 1:2 * d + 2])
            log_a = c8[d:d + 1] * r
            a = jnp.exp(log_a)
            a_scr[r0:r0 + tc, :] = a
            x_scr[r0:r0 + tc, :] = jnp.sqrt(1.0 - a * a) * (ig * u)

    rowi = lax.broadcasted_iota(I32, (SUBLANES, HEAD), 0)
    nblk = L // SUBLANES

    def scan_step(i, carry):
        h_fwd, h_bwd = carry
        r0 = pl.multiple_of(i * SUBLANES, SUBLANES)
        a = a_f[pl.ds(r0, SUBLANES), :]
        x = x_f[pl.ds(r0, SUBLANES), :]
        for s in (1, 2, 4):
            ok = rowi >= s
            a_s = jnp.where(ok, pltpu.roll(a, s, 0), 1.0)
            x_s = jnp.where(ok, pltpu.roll(x, s, 0), 0.0)
            x = a * x_s + x
            a = a * a_s
        x_f[pl.ds(r0, SUBLANES), :] = x + a * h_fwd
        h_fwd = (jnp.broadcast_to(x[SUBLANES - 1:SUBLANES], x.shape)
                 + jnp.broadcast_to(a[SUBLANES - 1:SUBLANES], a.shape) * h_fwd)

        r1 = pl.multiple_of((nblk - 1 - i) * SUBLANES, SUBLANES)
        a = a_b[pl.ds(r1, SUBLANES), :]
        x = x_b[pl.ds(r1, SUBLANES), :]
        for s in (1, 2, 4):
            ok = rowi < SUBLANES - s
            a_s = jnp.where(ok, pltpu.roll(a, SUBLANES - s, 0), 1.0)
            x_s = jnp.where(ok, pltpu.roll(x, SUBLANES - s, 0), 0.0)
            x = a * x_s + x
            a = a * a_s
        x_b[pl.ds(r1, SUBLANES), :] = x + a * h_bwd
        h_bwd = jnp.broadcast_to(x[0:1], x.shape) + jnp.broadcast_to(a[0:1], a.shape) * h_bwd
        return h_fwd, h_bwd

    zero_blk = jnp.zeros((SUBLANES, HEAD), F32)
    lax.fori_loop(0, nblk, scan_step, (zero_blk, zero_blk), unroll=8)
    g = g_ref[0]
    gelu = 0.5 * g * (1.0 + jnp.tanh(math.sqrt(2.0 / math.pi) * (g + 0.044715 * (g * g * g))))
    o_ref[0] = (x_f[...] + x_b[...]) * gelu


def _rglru_call(proj, cw, cb, wg, bg, c8):
    bn, L, _ = proj.shape
    nt = C_WIDTH // HEAD
    blk = (1, L, HEAD)
    scr = pltpu.VMEM((L, HEAD), F32)
    return pl.pallas_call(
        functools.partial(_rglru_kernel, L=L),
        grid=(bn, nt),
        in_specs=[
            pl.BlockSpec(blk, lambda b, j: (b, 0, COL_CX // HEAD + j)),
            pl.BlockSpec(blk, lambda b, j: (b, 0, COL_CG // HEAD + j)),
            pl.BlockSpec((4, HEAD), lambda b, j: (0, j)),
            pl.BlockSpec((1, HEAD), lambda b, j: (0, j)),
            pl.BlockSpec((1, HEAD, 4 * HEAD), lambda b, j: (j, 0, 0)),
            pl.BlockSpec((1, 4, HEAD), lambda b, j: (j, 0, 0)),
            pl.BlockSpec((1, 2, HEAD), lambda b, j: (j, 0, 0)),
        ],
        out_specs=pl.BlockSpec(blk, lambda b, j: (b, 0, j)),
        out_shape=jax.ShapeDtypeStruct((bn, L, C_WIDTH), F32),
        scratch_shapes=[pltpu.VMEM((L + 2 * SUBLANES, HEAD), F32), scr, scr, scr, scr],
        name="rglru",
        compiler_params=_cparams(("arbitrary", "arbitrary")),
    )(proj, proj, cw, cb, wg, bg, c8)


def _out_kernel(oa_ref, ob_ref, yc_ref, x_ref, gm_ref, rgn_ref, w_ref, npost_ref, npre_ref,
                sc_ref, sh_ref, wr_ref, x1_ref, h2_ref, pt_ref):
    c = (_rms(yc_ref[...], 1e-6) * rgn_ref[...]).astype(BF16)
    lhs = jnp.concatenate([oa_ref[...], ob_ref[...], c], axis=1)
    mix = jnp.dot(lhs, w_ref[...], preferred_element_type=F32)
    x1 = x_ref[...] + gm_ref[0] * (_rms(mix, 1e-6) * npost_ref[...])
    x1_ref[...] = x1
    h2 = _rms(x1, 1e-6) * npre_ref[...]
    h2 = h2 * (1.0 + sc_ref[0]) + sh_ref[0]
    hb = h2.astype(BF16)
    lo = lax.bitcast_convert_type(hb[:, :D_MODEL // 2].astype(F32), U32)
    hi = lax.bitcast_convert_type(hb[:, D_MODEL // 2:].astype(F32), U32)
    h2_ref[...] = hi | lax.shift_right_logical(lo, jnp.uint32(16))
    logits = _dot_nt(wr_ref[...], h2, precision=HIGHEST)
    m = jnp.max(logits, axis=0, keepdims=True)
    e = jnp.exp(logits - m)
    pt_ref[...] = e / jnp.sum(e, axis=0, keepdims=True)


def _out_call(oa, ob, yc, x, gm, rgn, w_bf16, npost, npre, sc, sh, wr_t, L):
    M = x.shape[0]
    tm = 512 if L % 512 == 0 else L
    lt = L // tm
    row = lambda w: pl.BlockSpec((tm, w), lambda i: (i, 0))
    vec = lambda w: pl.BlockSpec((1, w), lambda i: (0, 0))
    per_b = pl.BlockSpec((1, 1, D_MODEL), lambda i: (i // lt, 0, 0))
    return pl.pallas_call(
        _out_kernel,
        grid=(M // tm,),
        in_specs=[
            row(A_WIDTH), row(B_WIDTH), row(C_WIDTH), row(D_MODEL),
            per_b, vec(C_WIDTH),
            pl.BlockSpec((D_MODEL, D_MODEL), lambda i: (0, 0), pipeline_mode=pl.Buffered(1)),
            vec(D_MODEL), vec(D_MODEL), per_b, per_b,
            pl.BlockSpec((N_EXPERTS, D_MODEL), lambda i: (0, 0)),
        ],
        out_specs=[row(D_MODEL), row(D_MODEL // 2), pl.BlockSpec((N_EXPERTS, tm), lambda i: (0, i))],
        out_shape=[
            jax.ShapeDtypeStruct((M, D_MODEL), F32),
            jax.ShapeDtypeStruct((M, D_MODEL // 2), U32),
            jax.ShapeDtypeStruct((N_EXPERTS, M), F32),
        ],
        name="out_proj_router",
        compiler_params=_cparams(("arbitrary",)),
    )(oa, ob, yc, x, gm, rgn.reshape(1, C_WIDTH), w_bf16, npost.reshape(1, D_MODEL),
      npre.reshape(1, D_MODEL), sc, sh, wr_t)


def _select_kernel(p_ref, sel_ref, *, cap, n):
    bits = lax.bitcast_convert_type(p_ref[...], I32)
    idx = lax.broadcasted_iota(I32, bits.shape, 1)

    def count(mask):
        return jnp.sum(mask.astype(F32), axis=-1, keepdims=True).astype(I32)

    def value_step(i, ans):
        cand = ans | lax.shift_left(jnp.int32(1), 30 - i)
        return jnp.where(count(bits >= cand) >= cap, cand, ans)

    thr = lax.fori_loop(0, 31, value_step, jnp.zeros((N_EXPERTS, 1), I32))
    gt = bits > thr
    eq = bits == thr
    need = cap - count(gt)

    def index_step(i, lohi):
        lo, hi = lohi
        mid = lax.shift_right_arithmetic(lo + hi, 1)
        ok = count(eq & (idx <= mid)) >= need
        return jnp.where(ok, lo, mid + 1), jnp.where(ok, mid, hi)

    steps = max(1, (n - 1).bit_length())
    lo, _ = lax.fori_loop(0, steps, index_step,
                          (jnp.zeros((N_EXPERTS, 1), I32), jnp.full((N_EXPERTS, 1), n - 1, I32)))
    sel_ref[...] = (gt | (eq & (idx <= lo))).astype(I32)


def _select_call(probs_t, cap):
    n = probs_t.shape[1]
    return pl.pallas_call(
        functools.partial(_select_kernel, cap=cap, n=n),
        out_shape=jax.ShapeDtypeStruct((N_EXPERTS, n), I32),
        name="ec_select",
        compiler_params=pltpu.CompilerParams(vmem_limit_bytes=VMEM_LIMIT),
    )(probs_t)


def _compact_kernel(m_ref, p0_ref, p1_ref, excl_ref, c0_ref, c1_ref, *, rows, fill0, fill1):
    mask = m_ref[...]
    mask_f = mask.astype(F32)
    li = lax.broadcasted_iota(I32, (LANES, LANES), 0)
    lj = lax.broadcasted_iota(I32, (LANES, LANES), 1)
    upper = (li <= lj).astype(BF16)
    c_row = jnp.dot(mask.astype(BF16), upper, preferred_element_type=F32)
    rb = min(rows, 256)
    ri = lax.broadcasted_iota(I32, (rb, rb), 0)
    rj = lax.broadcasted_iota(I32, (rb, rb), 1)
    strict = (rj < ri).astype(BF16)
    carry = jnp.zeros((1, LANES), F32)
    offs = []
    for blk in range(rows // rb):
        tot = jnp.broadcast_to(c_row[blk * rb:(blk + 1) * rb, LANES - 1:LANES], (rb, LANES))
        pre = jnp.dot(strict, tot.astype(BF16), preferred_element_type=F32) + carry
        offs.append(pre)
        carry = pre[rb - 1:rb] + tot[rb - 1:rb]
    row_off = offs[0] if len(offs) == 1 else jnp.concatenate(offs, axis=0)
    excl = (row_off + c_row - mask_f).astype(I32)
    excl_ref[...] = excl

    lane = lax.broadcasted_iota(I32, (rows, LANES), 1)
    flat = lax.broadcasted_iota(I32, (rows, LANES), 0) * LANES + lane
    valid = mask
    disp = jnp.where(mask != 0, flat - excl, 0)
    pay0 = p0_ref[...]
    pay1 = p1_ref[...]
    nbits = (rows * LANES - 1).bit_length()
    for bit in range(nbits):
        s = 1 << bit
        if s < LANES:
            def shift(x, s=s):
                t = pltpu.roll(x, LANES - s, 1)
                t2 = pltpu.roll(t, rows - 1, 0)
                return jnp.where(lane < LANES - s, t, t2)
        else:
            def shift(x, s=s):
                return pltpu.roll(x, rows - s // LANES, 0)
        moving = valid & (lax.shift_right_logical(disp, bit) & 1)
        arrive = shift(moving) != 0
        disp = jnp.where(arrive, shift(disp), disp)
        pay0 = jnp.where(arrive, shift(pay0), pay0)
        pay1 = jnp.where(arrive, shift(pay1), pay1)
        valid = jnp.where(arrive, 1, valid & (1 - moving))
    c0_ref[...] = jnp.where(valid != 0, pay0, fill0)
    c1_ref[...] = jnp.where(valid != 0, pay1, fill1)


def _compact_call(mask, pay0, pay1, fill0, fill1):
    rows = mask.shape[0]
    out = jax.ShapeDtypeStruct((rows, LANES), I32)
    return pl.pallas_call(
        functools.partial(_compact_kernel, rows=rows, fill0=fill0, fill1=fill1),
        out_shape=[out, out, out],
        name="ec_compact",
        compiler_params=pltpu.CompilerParams(vmem_limit_bytes=VMEM_LIMIT),
    )(mask, pay0, pay1)


def _ffn_kernel(idx_ref, idx_next_ref, h_hbm, gate_ref, wg_ref, wu_ref, wd_ref, o_ref, buf, sem,
                *, tm, nsteps, nblk):
    step = pl.program_id(0) * nblk + pl.program_id(1)
    slot = lax.rem(step, 2)

    def issue(ref, dst_slot):
        for r in range(tm):
            pltpu.make_async_copy(h_hbm.at[pl.ds(ref[0, 0, r], 1), :], buf.at[dst_slot, pl.ds(r, 1), :],
                                  sem.at[dst_slot]).start()

    def wait(wait_slot):
        pltpu.make_async_copy(h_hbm.at[pl.ds(0, tm), :], buf.at[wait_slot], sem.at[wait_slot]).wait()

    @pl.when(step == 0)
    def _():
        issue(idx_ref, 0)

    wait(slot)
    u = buf[slot]
    x_lo = lax.bitcast_convert_type(lax.shift_left(u, jnp.uint32(16)), F32).astype(BF16)
    x_hi = lax.bitcast_convert_type(u & jnp.uint32(0xFFFF0000), F32).astype(BF16)
    x = jnp.concatenate([x_lo, x_hi], axis=1)
    issue(idx_next_ref, 1 - slot)
    hg = jnp.dot(x, wg_ref[0], preferred_element_type=F32)
    hu = jnp.dot(x, wu_ref[0], preferred_element_type=F32)
    hid = (hg * _sigmoid(hg) * hu).astype(BF16)
    y = jnp.dot(hid, wd_ref[0], preferred_element_type=F32)
    o_ref[...] = y * gate_ref[...]

    @pl.when(step == nsteps - 1)
    def _():
        wait(1 - slot)


def _ffn_call(idx, h2, gates, wg, wu, wd, tm):
    n_e, slots = idx.shape
    nblk = slots // tm
    nsteps = n_e * nblk
    idx3 = idx.reshape(nsteps, 1, tm)

    def nxt(e, j):
        lin = jnp.minimum(e * nblk + j + 1, nsteps - 1)
        return (lin, 0, 0)

    return pl.pallas_call(
        functools.partial(_ffn_kernel, tm=tm, nsteps=nsteps, nblk=nblk),
        grid=(n_e, nblk),
        in_specs=[
            pl.BlockSpec((1, 1, tm), lambda e, j: (e * nblk + j, 0, 0), memory_space=pltpu.SMEM),
            pl.BlockSpec((1, 1, tm), nxt, memory_space=pltpu.SMEM),
            pl.BlockSpec(memory_space=pl.ANY),
            pl.BlockSpec((tm, 1), lambda e, j: (e * nblk + j, 0)),
            pl.BlockSpec((1, D_MODEL, D_EXPERT), lambda e, j: (e, 0, 0), pipeline_mode=pl.Buffered(1)),
            pl.BlockSpec((1, D_MODEL, D_EXPERT), lambda e, j: (e, 0, 0), pipeline_mode=pl.Buffered(1)),
            pl.BlockSpec((1, D_EXPERT, D_MODEL), lambda e, j: (e, 0, 0), pipeline_mode=pl.Buffered(1)),
        ],
        out_specs=pl.BlockSpec((tm, D_MODEL), lambda e, j: (e * nblk + j, 0)),
        out_shape=jax.ShapeDtypeStruct((n_e * slots, D_MODEL), F32),
        scratch_shapes=[pltpu.VMEM((2, tm, D_MODEL // 2), U32), pltpu.SemaphoreType.DMA((2,))],
        name="ec_ffn",
        compiler_params=_cparams(("arbitrary", "arbitrary")),
    )(idx3, idx3, h2, gates, wg, wu, wd)


CMB_VALID, CMB_FIRST, CMB_LAST, CMB_NEWWIN = 1, 2, 4, 8


def _combine_schedule(off, nwin, win):
    ntiles = off.shape[0] - 1
    lo = jnp.minimum(off[:-1] // win, nwin - 1)
    hi = jnp.maximum(lo, jnp.minimum((off[1:] - 1) // win, nwin - 1))
    cnt = hi - lo + 1
    start = jnp.cumsum(cnt) - cnt
    total = start[-1] + cnt[-1]
    k = jnp.arange(ntiles + nwin, dtype=I32)
    valid = k < total
    t = jnp.clip(jnp.sum((start[None, :] <= k[:, None]).astype(I32), axis=1) - 1, 0, ntiles - 1)
    t = jnp.where(valid, t, ntiles - 1)
    w = jnp.where(valid, lo[t] + (k - start[t]), hi[-1])
    first = valid & (k == start[t])
    last = valid & (k == start[t] + cnt[t] - 1)
    neww = valid & (w != jnp.concatenate([jnp.full((1,), -1, I32), w[:-1]]))
    flags = (valid * CMB_VALID + first * CMB_FIRST + last * CMB_LAST + neww * CMB_NEWWIN).astype(I32)
    return t, w.astype(I32), flags


def _combine_kernel(tile_ref, win_ref, flag_ref, src_ref, src_next_ref, tok_ref, ye_hbm, x1_ref,
                    gf_ref, nw_ref, *rest, tt, win, nitems, split_tile):
    if split_tile is None:
        (o_ref,), (zbuf, z_hi, z_lo, acc, sem) = rest[:1], rest[1:]
    else:
        (o_ref, o2_ref), (zbuf, z_hi, z_lo, acc, sem) = rest[:2], rest[2:]
    k = pl.program_id(0)
    flags = flag_ref[k]
    w = win_ref[k]
    slot = lax.rem(w, 2)
    new_window = (flags & CMB_NEWWIN) != 0

    def issue(ref, dst_slot):
        for r in range(win):
            pltpu.make_async_copy(ye_hbm.at[pl.ds(ref[0, 0, r], 1), :], zbuf.at[dst_slot, pl.ds(r, 1), :],
                                  sem.at[dst_slot]).start()

    def wait(wait_slot):
        pltpu.make_async_copy(ye_hbm.at[pl.ds(0, win), :], zbuf.at[wait_slot], sem.at[wait_slot]).wait()

    def accumulate(hi, lo):
        tok_col = tile_ref[k] * tt + lax.broadcasted_iota(I32, (tt, 1), 0)
        seg = jnp.concatenate([(tok_ref[0, kk:kk + 1, :] == tok_col).astype(BF16)
                               for kk in range(win // LANES)], axis=1)
        prev = jnp.where((flags & CMB_FIRST) != 0, 0.0, acc[...])
        acc[...] = (prev + jnp.dot(seg, hi, preferred_element_type=F32)
                    + jnp.dot(seg, lo, preferred_element_type=F32))

    @pl.when(k == 0)
    def _():
        issue(src_ref, slot)

    @pl.when(new_window)
    def _():
        wait(slot)
        issue(src_next_ref, 1 - slot)
        z = zbuf[slot]
        hi = z.astype(BF16)
        lo = (z - hi.astype(F32)).astype(BF16)
        z_hi[...] = hi
        z_lo[...] = lo
        accumulate(hi, lo)

    @pl.when(((flags & CMB_VALID) != 0) & jnp.logical_not(new_window))
    def _():
        accumulate(z_hi[...], z_lo[...])

    @pl.when((flags & CMB_LAST) != 0)
    def _():
        out = x1_ref[...] + gf_ref[0] * (_rms(acc[...], 1e-6) * nw_ref[...])
        if split_tile is None:
            o_ref[...] = out
        else:
            @pl.when(tile_ref[k] < split_tile)
            def _():
                o_ref[...] = out

            @pl.when(tile_ref[k] >= split_tile)
            def _():
                o2_ref[...] = out

    @pl.when(k == nitems - 1)
    def _():
        wait(1 - slot)


def _combine_call(off, src, tok, ye, x1, gf, nw, L, tt, win, split_rows=None):
    M = x1.shape[0]
    lt = L // tt
    nwin = src.shape[0] // win
    tile_k, win_k, flag_k = _combine_schedule(off, nwin, win)
    nitems = tile_k.shape[0]
    nxt = lambda k, t, w, f: (jnp.minimum(w[k] + 1, nwin - 1), 0, 0)
    if split_rows is None:
        split_tile = None
        out_specs = pl.BlockSpec((tt, D_MODEL), lambda k, t, w, f: (t[k], 0))
        out_shape = jax.ShapeDtypeStruct((M, D_MODEL), F32)
    else:
        split_tile = split_rows // tt
        out_specs = [
            pl.BlockSpec((tt, D_MODEL), lambda k, t, w, f: (jnp.minimum(t[k], split_tile - 1), 0)),
            pl.BlockSpec((tt, D_MODEL), lambda k, t, w, f: (jnp.maximum(t[k] - split_tile, 0), 0)),
        ]
        out_shape = [jax.ShapeDtypeStruct((split_rows, D_MODEL), F32),
                     jax.ShapeDtypeStruct((M - split_rows, D_MODEL), F32)]
    grid_spec = pltpu.PrefetchScalarGridSpec(
        num_scalar_prefetch=3,
        grid=(nitems,),
        in_specs=[
            pl.BlockSpec((1, 1, win), lambda k, t, w, f: (w[k], 0, 0), memory_space=pltpu.SMEM),
            pl.BlockSpec((1, 1, win), nxt, memory_space=pltpu.SMEM),
            pl.BlockSpec((1, win // LANES, LANES), lambda k, t, w, f: (w[k], 0, 0)),
            pl.BlockSpec(memory_space=pl.ANY),
            pl.BlockSpec((tt, D_MODEL), lambda k, t, w, f: (t[k], 0)),
            pl.BlockSpec((1, 1, D_MODEL), lambda k, t, w, f: (t[k] // lt, 0, 0)),
            pl.BlockSpec((1, D_MODEL), lambda k, t, w, f: (0, 0)),
        ],
        out_specs=out_specs,
        scratch_shapes=[
            pltpu.VMEM((2, win, D_MODEL), F32),
            pltpu.VMEM((win, D_MODEL), BF16),
            pltpu.VMEM((win, D_MODEL), BF16),
            pltpu.VMEM((tt, D_MODEL), F32),
            pltpu.SemaphoreType.DMA((2,)),
        ],
    )
    src3 = src.reshape(nwin, 1, win)
    return pl.pallas_call(
        functools.partial(_combine_kernel, tt=tt, win=win, nitems=nitems, split_tile=split_tile),
        grid_spec=grid_spec,
        out_shape=out_shape,
        name="ec_combine",
        compiler_params=_cparams(("arbitrary",)),
    )(tile_k, win_k, flag_k, src3, src3, tok.reshape(nwin, win // LANES, LANES), ye, x1, gf,
      nw.reshape(1, D_MODEL))


def _route_group(probs_t, tok_base, slot_base, slots_total, tt):
    n = probs_t.shape[1]
    cap = max(1, EC_FACTOR * n // N_EXPERTS)
    rows = N_EXPERTS * n // LANES
    sel = _select_call(probs_t, cap)
    tok_ids = tok_base + lax.broadcasted_iota(I32, (N_EXPERTS, n), 1)
    excl_e, idx_c, gate_c = _compact_call(
        sel.reshape(rows, LANES), tok_ids.reshape(rows, LANES),
        lax.bitcast_convert_type(probs_t, I32).reshape(rows, LANES), tok_base, 0)
    npair = N_EXPERTS * cap
    idx_e = idx_c.reshape(-1)[:npair].reshape(N_EXPERTS, cap)
    gate_e = lax.bitcast_convert_type(gate_c.reshape(-1)[:npair], F32).reshape(N_EXPERTS, cap)
    e_col = lax.broadcasted_iota(I32, (N_EXPERTS, n), 0)
    src = excl_e.reshape(N_EXPERTS, n) - e_col * cap + e_col * slots_total + slot_base
    excl_t, src_c, tok_c = _compact_call(
        sel.T.reshape(rows, LANES), src.T.reshape(rows, LANES),
        tok_ids.T.reshape(rows, LANES), 0, -1)
    src_t = src_c.reshape(-1)[:npair]
    tok_t = tok_c.reshape(-1)[:npair]
    off = excl_t.reshape(-1)[::tt * N_EXPERTS]
    return idx_e, gate_e, src_t, tok_t, off, npair


def _block_diag_tiles(w):
    nt = C_WIDTH // HEAD
    per = HEAD // C_BLOCK
    w = w.reshape(2, nt, per, C_BLOCK, C_BLOCK)
    eye = jnp.eye(per, dtype=w.dtype)
    t = jnp.einsum('dtpce,pq->dtpcqe', w, eye)
    return t.reshape(2, nt, HEAD, HEAD).transpose(1, 0, 2, 3)


def kernel(x_prompt, x_sample, c_prompt, c_sample, ada_w, ada_b, norm_mix_pre, norm_mix_post, norm_ffn_pre, norm_ffn_post, w_in, hg_lower, hg_norm, dl_q1, dl_k1, dl_q2, dl_k2, dl_subln, conv_w, conv_b, rg_wa, rg_ba, rg_wx, rg_bx, rg_lambda, rg_norm, w_out, w_router, w_gate, w_up, w_down):
    bp, L, D = x_prompt.shape
    bs = x_sample.shape[0]
    bn = bp + bs
    n_p, n_s = bp * L, bs * x_sample.shape[1]
    M = n_p + n_s
    x = jnp.concatenate([x_prompt, x_sample], axis=0)
    c = jnp.concatenate([c_prompt, c_sample], axis=0)
    mod = _ada_call(c, ada_w, ada_b)

    lb_soft = jax.nn.softmax(hg_lower.astype(F32), axis=0)
    lb_all = jnp.cumsum(lb_soft, axis=0) - lb_soft[0:1]
    half = ROT_DIM // 2
    inv_freq = ROPE_THETA ** (-jnp.arange(half, dtype=F32) / half)
    ang = jnp.arange(L, dtype=F32)[:, None] * inv_freq[None, :]
    cos, sin = jnp.cos(ang), jnp.sin(ang)
    one = jnp.ones((L, B_DQK - ROT_DIM), F32)
    zero = jnp.zeros((L, B_DQK - ROT_DIM), F32)
    zh = jnp.zeros((L, half), F32)
    cos_t = jnp.tile(jnp.concatenate([cos, cos, one], axis=1), (1, 2))
    s1_t = jnp.tile(jnp.concatenate([-sin, zh, zero], axis=1), (1, 2))
    s2_t = jnp.tile(jnp.concatenate([zh, sin, zero], axis=1), (1, 2))

    cap_p = max(1, EC_FACTOR * n_p // N_EXPERTS)
    cap_s = max(1, EC_FACTOR * n_s // N_EXPERTS)
    slots_total = cap_p + cap_s
    tm_ffn = math.gcd(512, math.gcd(cap_p, cap_s))
    tt = 256 if L % 256 == 0 else L
    win = 256

    xf = x
    for l in range(DEPTH):
        m6 = mod[l].reshape(bn, N_MOD, 1, D)
        sh_m, sc_m, g_m, sh_f, sc_f, g_f = (m6[:, i] for i in range(N_MOD))

        proj = _in_call(xf, norm_mix_pre[l], sc_m, sh_m, w_in[l].astype(BF16))

        lb = lb_all[l].reshape(A_HEADS, HEAD)
        par = jnp.stack([jnp.maximum(lb, LB_MIN), 1.0 - lb], axis=1)
        o_a = _hgrn_call(proj, par, hg_norm[l])

        lam_init = 0.8 - 0.6 * math.exp(-0.3 * l)
        lam = (jnp.exp(jnp.sum(dl_q1[l].astype(F32) * dl_k1[l].astype(F32)))
               - jnp.exp(jnp.sum(dl_q2[l].astype(F32) * dl_k2[l].astype(F32))) + lam_init)
        lam2 = jnp.stack([lam, jnp.asarray(1.0 - lam_init, F32)])
        qb, kb, vb = _qkv_call(proj, cos_t, s1_t, s2_t)
        o_b = _attn_call(lam2, qb, kb, vb, dl_subln[l])

        nt = C_WIDTH // HEAD
        wa_t = _block_diag_tiles(rg_wa[l])
        wx_t = _block_diag_tiles(rg_wx[l])
        wg = jnp.concatenate([wa_t[:, 0], wx_t[:, 0], wa_t[:, 1], wx_t[:, 1]], axis=-1).astype(BF16)
        bg = jnp.stack([rg_ba[l, 0], rg_bx[l, 0], rg_ba[l, 1], rg_bx[l, 1]], axis=0)
        bg = bg.reshape(4, nt, HEAD).transpose(1, 0, 2)
        c8 = (-RG_C * jax.nn.softplus(-rg_lambda[l])).reshape(2, nt, HEAD).transpose(1, 0, 2)
        y_c = _rglru_call(proj, conv_w[l], conv_b[l].reshape(1, C_WIDTH), wg, bg, c8)

        x1, h2, probs_t = _out_call(
            o_a.reshape(M, A_WIDTH), o_b.reshape(M, B_WIDTH), y_c.reshape(M, C_WIDTH),
            xf.reshape(M, D), g_m, rg_norm[l], w_out[l].astype(BF16), norm_mix_post[l],
            norm_ffn_pre[l], sc_f, sh_f, w_router[l].T, L)

        ie_p, ge_p, src_p, tok_p, off_p, np_p = _route_group(probs_t[:, :n_p], 0, 0, slots_total, tt)
        ie_s, ge_s, src_s, tok_s, off_s, np_s = _route_group(probs_t[:, n_p:], n_p, cap_p, slots_total, tt)
        idx = jnp.concatenate([ie_p, ie_s], axis=1)
        gates = jnp.concatenate([ge_p, ge_s], axis=1).reshape(N_EXPERTS * slots_total, 1)
        ye = _ffn_call(idx, h2, gates, w_gate[l].astype(BF16), w_up[l].astype(BF16),
                       w_down[l].astype(BF16), tm_ffn)

        pad = -(np_p + np_s) % win
        src = jnp.concatenate([src_p, src_s, jnp.zeros((pad,), I32)])
        tok = jnp.concatenate([tok_p, tok_s, jnp.full((pad,), -1, I32)])
        off = jnp.concatenate([off_p, off_s + np_p, jnp.full((1,), np_p + np_s, I32)])
        if l + 1 < DEPTH:
            x2 = _combine_call(off, src, tok, ye, x1, g_f, norm_ffn_post[l], L, tt, win)
            xf = x2.reshape(bn, L, D)
        else:
            y_p, y_s = _combine_call(off, src, tok, ye, x1, g_f, norm_ffn_post[l], L, tt, win,
                                     split_rows=n_p)
    return y_p.reshape(x_prompt.shape), y_s.reshape(x_sample.shape)
```

```python
import functools
import math

import jax
import jax.numpy as jnp
from jax import lax
from jax.experimental import pallas as pl
from jax.experimental.pallas import tpu as pltpu

F32 = jnp.float32
BF16 = jnp.bfloat16
I32 = jnp.int32
U32 = jnp.uint32
HIGHEST = lax.Precision.HIGHEST

D_MODEL = 2048
DEPTH = 2
A_WIDTH = D_MODEL // 4
B_WIDTH = D_MODEL // 2
C_WIDTH = D_MODEL // 4
HEAD = 128
A_HEADS = A_WIDTH // HEAD
B_HEADS = B_WIDTH // HEAD
B_DQK = HEAD // 2
ROT_DIM = B_DQK // 4
ROPE_THETA = 500000.0
LB_MIN = 1e-12
C_BLOCKS = 8
C_BLOCK = C_WIDTH // C_BLOCKS
RG_C = 8.0
N_EXPERTS = 16
EC_FACTOR = 2
D_EXPERT = D_MODEL // 2
N_MOD = 6
IN_COLS = 3 * A_WIDTH + 2 * A_WIDTH + 3 * B_WIDTH + 2 * C_WIDTH
COL_AQ, COL_AFF, COL_AFB, COL_AI, COL_AG = 0, 512, 1024, 1536, 2048
COL_BQ, COL_BK, COL_BV = 2560, 3584, 4608
COL_CX, COL_CG = 5632, 6144

LANES = 128
SUBLANES = 8
VMEM_LIMIT = 56 * 1024 * 1024

HGRN_CHUNK = 64
HGRN_SUB = 16
HGRN_BLOCK = 256
HGRN_SAFE_RANGE = 80.0
ATTN_TQ = 1024
ATTN_SUB = 256


def _cparams(sem):
    return pltpu.CompilerParams(dimension_semantics=sem, vmem_limit_bytes=VMEM_LIMIT)


def _sigmoid(x):
    return 1.0 / (1.0 + jnp.exp(-x))


def _rms(x, eps):
    return x * lax.rsqrt(jnp.mean(x * x, axis=-1, keepdims=True) + eps)


def _dot_nt(a, b, **kw):
    return lax.dot_general(a, b, (((1,), (1,)), ((), ())), preferred_element_type=F32, **kw)


def _dot_tn(a, b):
    return lax.dot_general(a, b, (((0,), (0,)), ((), ())), preferred_element_type=F32)


CAST_BLOCK_BYTES = 8 * 1024 * 1024


def _cast_kernel(x_ref, o_ref):
    o_ref[...] = x_ref[...].astype(o_ref.dtype)


def _cast_call(w):
    cols = w.shape[-1]
    rows = w.size // cols
    rb = max(SUBLANES, min(rows, CAST_BLOCK_BYTES // (4 * cols) // SUBLANES * SUBLANES))
    while rows % rb:
        rb -= SUBLANES
    spec = pl.BlockSpec((rb, cols), lambda i: (i, 0))
    out = pl.pallas_call(
        _cast_kernel,
        grid=(rows // rb,),
        in_specs=[spec],
        out_specs=spec,
        out_shape=jax.ShapeDtypeStruct((rows, cols), BF16),
        name="cast_bf16",
        compiler_params=_cparams(("arbitrary",)),
    )(w.reshape(rows, cols))
    return out.reshape(w.shape)


def _ada_kernel(c_ref, w_ref, b_ref, o_ref):
    c = c_ref[...]
    a = (c * _sigmoid(c)).astype(BF16)
    o_ref[0] = jnp.dot(a, w_ref[0].astype(BF16), preferred_element_type=F32) + b_ref[0]


def _ada_call(c, ada_w, ada_b):
    bn = c.shape[0]
    tn = 1024
    ncol = N_MOD * D_MODEL
    return pl.pallas_call(
        _ada_kernel,
        grid=(DEPTH, ncol // tn),
        in_specs=[
            pl.BlockSpec((bn, D_MODEL), lambda l, j: (0, 0)),
            pl.BlockSpec((1, D_MODEL, tn), lambda l, j: (l, 0, j)),
            pl.BlockSpec((1, 1, tn), lambda l, j: (l, 0, j)),
        ],
        out_specs=pl.BlockSpec((1, bn, tn), lambda l, j: (l, 0, j)),
        out_shape=jax.ShapeDtypeStruct((DEPTH, bn, ncol), F32),
        name="ada_mod",
        compiler_params=_cparams(("arbitrary", "arbitrary")),
    )(c, ada_w, ada_b.reshape(DEPTH, 1, ncol))


IN_TN = 512
IN_B0, IN_B1 = COL_BQ // IN_TN, COL_CX // IN_TN
IN_ROPE = (COL_BV - COL_BQ) // IN_TN
IN_Q = (COL_BK - COL_BQ) // IN_TN
IN_ROWS = 128


def _in_kernel(x_ref, nw_ref, sc_ref, sh_ref, w_ref, cos_ref, s1_ref, s2_ref, ac_ref, b_ref, h_scr):
    j = pl.program_id(1)

    @pl.when(j == 0)
    def _():
        def rows(c, carry):
            r = pl.ds(pl.multiple_of(c * IN_ROWS, IN_ROWS), IN_ROWS)
            h = _rms(x_ref[0, r, :], 1e-6) * nw_ref[...]
            h_scr[r, :] = (h * (1.0 + sc_ref[0]) + sh_ref[0]).astype(BF16)
            return carry
        lax.fori_loop(0, h_scr.shape[0] // IN_ROWS, rows, 0)

    acc = jnp.dot(h_scr[...], w_ref[0], preferred_element_type=F32)
    in_b = (j >= IN_B0) & (j < IN_B1)

    @pl.when(jnp.logical_not(in_b))
    def _():
        ac_ref[0] = acc

    @pl.when(in_b & (j < IN_B0 + IN_ROPE))
    def _():
        mul = jnp.where(j < IN_B0 + IN_Q, B_DQK ** -0.5 * math.log2(math.e), 1.0)
        cos = cos_ref[...]
        s1 = s1_ref[...]
        s2 = s2_ref[...]
        for h in range(IN_TN // HEAD):
            sl = slice(HEAD * h, HEAD * (h + 1))
            x = acc[:, sl]
            xr = x * cos + pltpu.roll(x, HEAD - ROT_DIM // 2, 1) * s1 + pltpu.roll(x, ROT_DIM // 2, 1) * s2
            b_ref[0, :, sl] = (xr * mul).astype(BF16)

    @pl.when(in_b & (j >= IN_B0 + IN_ROPE))
    def _():
        b_ref[0] = acc.astype(BF16)


def _in_call(x, nw, sc, sh, w_all, layer, cos_t, s1_t, s2_t):
    bn, L, _ = x.shape
    tm = min(1024, L)
    tn = IN_TN
    lt = L // tm
    nb = IN_B1 - IN_B0
    tspec = pl.BlockSpec((tm, HEAD), lambda i, j: (i % lt, 0))
    ac_col = lambda j: jnp.where(j < IN_B0, j, jnp.maximum(j - nb, IN_B0 - 1))
    b_col = lambda j: jnp.clip(j - IN_B0, 0, nb - 1)
    return pl.pallas_call(
        _in_kernel,
        grid=(bn * lt, IN_COLS // tn),
        in_specs=[
            pl.BlockSpec((1, tm, D_MODEL), lambda i, j: (i // lt, i % lt, 0)),
            pl.BlockSpec((1, D_MODEL), lambda i, j: (0, 0)),
            pl.BlockSpec((1, 1, D_MODEL), lambda i, j: (i // lt, 0, 0)),
            pl.BlockSpec((1, 1, D_MODEL), lambda i, j: (i // lt, 0, 0)),
            pl.BlockSpec((1, D_MODEL, tn), lambda i, j: (layer, 0, j)),
            tspec, tspec, tspec,
        ],
        out_specs=[
            pl.BlockSpec((1, tm, tn), lambda i, j: (i // lt, i % lt, ac_col(j))),
            pl.BlockSpec((1, tm, tn), lambda i, j: (i // lt, i % lt, b_col(j))),
        ],
        out_shape=[
            jax.ShapeDtypeStruct((bn, L, IN_COLS - 3 * B_WIDTH), F32),
            jax.ShapeDtypeStruct((bn, L, 3 * B_WIDTH), BF16),
        ],
        scratch_shapes=[pltpu.VMEM((tm, D_MODEL), BF16)],
        name="in_proj",
        compiler_params=_cparams(("arbitrary", "arbitrary")),
    )(x, nw.reshape(1, D_MODEL), sc, sh, w_all, cos_t, s1_t, s2_t)


def _sigmoid_pair(z):
    e = jnp.exp(-jnp.abs(z))
    r = 1.0 / (1.0 + e)
    er = e * r
    pos = z >= 0.0
    return jnp.where(pos, r, er), jnp.where(pos, er, r)


def _split3(x):
    hi = x.astype(BF16)
    r = x - hi.astype(F32)
    mid = r.astype(BF16)
    lo = (r - mid.astype(F32)).astype(BF16)
    return hi, mid, lo


def _hgrn_kernel(q_ref, ff_ref, fb_ref, i_ref, g_ref, par_ref, nw_ref, o_ref,
                 qs_scr, b_scr, k_scr, acc_scr, qe_scr, u_scr, dec_scr, st_scr, *, L):
    C, SB = HGRN_CHUNK, HGRN_SUB
    BLK = min(HGRN_BLOCK, L)
    cpb = BLK // C
    nblk = L // BLK
    nchunk = L // C
    nsb = C // SB
    lb_floor = par_ref[0, 0:1, :]
    one_m_lb = par_ref[0, 1:2, :]
    brow = lax.broadcasted_iota(I32, (BLK, BLK), 0)
    bcol = lax.broadcasted_iota(I32, (BLK, BLK), 1)
    same_chunk = (brow // C) == (bcol // C)
    sub_row = lax.broadcasted_iota(I32, (SB, 1), 0)

    ql = q_ref[0]
    qs_scr[...] = ql * _sigmoid_pair(ql)[0]

    def run_dir(f_ref, rev, first):
        causal = same_chunk & ((bcol >= brow) if rev else (bcol <= brow))
        tri = causal.astype(BF16)
        mid_off = C // 2
        end_off = 0 if rev else C - 1
        beg_off = C - 1 if rev else 0

        def gates(bi, rng):
            r0 = pl.multiple_of(bi * BLK, BLK)
            z = f_ref[0, pl.ds(r0, BLK), :]
            sig, sig_neg = _sigmoid_pair(z)
            log_f = jnp.log(lb_floor + one_m_lb * sig)
            k_scr[pl.ds(r0, BLK), :] = one_m_lb * sig_neg
            hi, mid, lo = _split3(log_f)
            b = (jnp.dot(tri, hi, preferred_element_type=F32)
                 + jnp.dot(tri, mid, preferred_element_type=F32)
                 + jnp.dot(tri, lo, preferred_element_type=F32))
            b_scr[pl.ds(r0, BLK), :] = b
            for c in range(cpb):
                m = b[c * C + mid_off:c * C + mid_off + 1]
                rng = jnp.maximum(rng, b[c * C + beg_off:c * C + beg_off + 1] - m)
                rng = jnp.maximum(rng, m - b[c * C + end_off:c * C + end_off + 1])
            return rng

        rng = lax.fori_loop(0, nblk, gates, jnp.zeros((1, HEAD), F32), unroll=2)
        safe = jnp.max(rng) <= HGRN_SAFE_RANGE

        @pl.when(safe)
        def _():
            def intra(bi, carry):
                r0 = pl.multiple_of(bi * BLK, BLK)
                b = b_scr[pl.ds(r0, BLK), :]
                k = k_scr[pl.ds(r0, BLK), :]
                q = qs_scr[pl.ds(r0, BLK), :]
                v_bf = i_ref[0, pl.ds(r0, BLK), :].astype(BF16)
                m = jnp.concatenate(
                    [jnp.broadcast_to(b[c * C + mid_off:c * C + mid_off + 1], (C, HEAD)) for c in range(cpb)], axis=0)
                b_end = jnp.concatenate(
                    [jnp.broadcast_to(b[c * C + end_off:c * C + end_off + 1], (C, HEAD)) for c in range(cpb)], axis=0)
                qt = (q * jnp.exp(b - m)).astype(BF16)
                kt = (k * jnp.exp(m - b)).astype(BF16)
                sc = _dot_nt(qt, kt)
                sc = jnp.where(causal, sc, 0.0).astype(BF16)
                o_intra = jnp.dot(sc, v_bf, preferred_element_type=F32)
                if first:
                    acc_scr[pl.ds(r0, BLK), :] = o_intra
                else:
                    acc_scr[pl.ds(r0, BLK), :] = acc_scr[pl.ds(r0, BLK), :] + o_intra
                qe_scr[pl.ds(r0, BLK), :] = (q * jnp.exp(b)).astype(BF16)
                kend = (k * jnp.exp(b_end - b)).astype(BF16)
                for c in range(cpb):
                    rows = slice(c * C, (c + 1) * C)
                    u_scr[bi * cpb + c] = _dot_tn(v_bf[rows], kend[rows])
                    dec_scr[pl.ds(bi * cpb + c, 1), :] = jnp.exp(b[c * C + end_off:c * C + end_off + 1])
                return carry

            lax.fori_loop(0, nblk, intra, 0, unroll=2)

            def inter(ci, st):
                c = (nchunk - 1 - ci) if rev else ci
                r0 = pl.multiple_of(c * C, C)
                o_state = _dot_nt(qe_scr[pl.ds(r0, C), :], st.astype(BF16))
                acc_scr[pl.ds(r0, C), :] = acc_scr[pl.ds(r0, C), :] + o_state
                return st * dec_scr[pl.ds(c, 1), :] + u_scr[c]

            lax.fori_loop(0, nchunk, inter, jnp.zeros((HEAD, HEAD), F32), unroll=4)

        @pl.when(jnp.logical_not(safe))
        def _():
            order = list(range(nsb - 1, -1, -1)) if rev else list(range(nsb))
            st_scr[...] = jnp.zeros_like(st_scr)

            def body(ci, carry):
                c = (nchunk - 1 - ci) if rev else ci
                r0 = pl.multiple_of(c * C, C)
                b = b_scr[pl.ds(r0, C), :]
                k = k_scr[pl.ds(r0, C), :]
                q = qs_scr[pl.ds(r0, C), :]
                v = i_ref[0, pl.ds(r0, C), :]
                st = st_scr[...]
                o_state = _dot_nt((q * jnp.exp(b)).astype(BF16), st.astype(BF16))
                v_bf = v.astype(BF16)
                for p, blk in enumerate(order):
                    lo = SB * blk
                    b_blk = b[lo:lo + SB]
                    q_blk = q[lo:lo + SB]
                    k_blk = k[lo:lo + SB]
                    v_blk = v[lo:lo + SB]
                    out = o_state[lo:lo + SB]
                    if p > 0:
                        if rev:
                            bound = b[lo + SB:lo + SB + 1]
                            e0, e1 = lo + SB, C
                        else:
                            bound = b[lo - 1:lo]
                            e0, e1 = 0, lo
                        qt = (q_blk * jnp.exp(b_blk - bound)).astype(BF16)
                        kt = (k[e0:e1] * jnp.exp(bound - b[e0:e1])).astype(BF16)
                        sc = _dot_nt(qt, kt)
                        out = out + jnp.dot(sc.astype(BF16), v_bf[e0:e1], preferred_element_type=F32)
                    diag = jnp.zeros((SB, HEAD), F32)
                    for t in range(SB):
                        bt = b_blk[t:t + 1]
                        pm = k_blk * jnp.exp(jnp.minimum(bt - b_blk, 0.0)) * q_blk[t:t + 1]
                        s = jnp.sum(pm, axis=-1, keepdims=True)
                        keep = (sub_row >= t) if rev else (sub_row <= t)
                        s = jnp.where(keep, s, 0.0)
                        o_t = jnp.sum(s * v_blk, axis=0, keepdims=True)
                        diag = jnp.where(sub_row == t, o_t, diag)
                    out = out + diag
                    rows = pl.ds(r0 + lo, SB)
                    if first:
                        acc_scr[rows, :] = out
                    else:
                        acc_scr[rows, :] = acc_scr[rows, :] + out
                b_end = b[0:1] if rev else b[C - 1:C]
                kend = (k * jnp.exp(b_end - b)).astype(BF16)
                st_scr[...] = st * jnp.exp(b_end) + _dot_tn(v_bf, kend)
                return carry

            lax.fori_loop(0, nchunk, body, 0)

    run_dir(ff_ref, False, True)
    run_dir(fb_ref, True, False)
    g = g_ref[0]
    o_ref[0] = (_rms(acc_scr[...], 1e-6) * nw_ref[...] * (g * _sigmoid(g))).astype(o_ref.dtype)


def _hgrn_call(proj, par, nw):
    bn, L, _ = proj.shape
    cb = lambda off: (lambda b, h: (b, 0, off // HEAD + h))
    blk = (1, L, HEAD)
    nchunk = L // HGRN_CHUNK
    seq = pltpu.VMEM((L, HEAD), F32)
    return pl.pallas_call(
        functools.partial(_hgrn_kernel, L=L),
        grid=(bn, A_HEADS),
        in_specs=[
            pl.BlockSpec(blk, cb(COL_AQ)),
            pl.BlockSpec(blk, cb(COL_AFF)),
            pl.BlockSpec(blk, cb(COL_AFB)),
            pl.BlockSpec(blk, cb(COL_AI)),
            pl.BlockSpec(blk, cb(COL_AG)),
            pl.BlockSpec((1, 2, HEAD), lambda b, h: (h, 0, 0)),
            pl.BlockSpec((1, HEAD), lambda b, h: (0, 0)),
        ],
        out_specs=pl.BlockSpec(blk, lambda b, h: (b, 0, h)),
        out_shape=jax.ShapeDtypeStruct((bn, L, A_WIDTH), BF16),
        scratch_shapes=[
            seq, seq, seq, seq,
            pltpu.VMEM((L, HEAD), BF16),
            pltpu.VMEM((nchunk, HEAD, HEAD), F32),
            pltpu.VMEM((nchunk, HEAD), F32),
            pltpu.VMEM((HEAD, HEAD), F32),
        ],
        name="hgrn2",
        compiler_params=_cparams(("arbitrary", "arbitrary")),
    )(proj, proj, proj, proj, proj, par, nw.reshape(1, HEAD))


def _attn_kernel(lam_ref, q_ref, k_ref, v_ref, sw_ref, o_ref, vx_scr):
    @pl.when(pl.program_id(2) == 0)
    def _():
        lane = lax.broadcasted_iota(I32, (vx_scr.shape[0], HEAD), 1)
        vx_scr[:, 0:HEAD] = v_ref[0]
        vx_scr[:, HEAD:2 * HEAD] = jnp.where(lane == 0, 1.0, 0.0).astype(BF16)

    lam = lam_ref[0]
    post = lam_ref[1]
    k = k_ref[0]
    lane = lax.broadcasted_iota(I32, (1, HEAD), 1)
    sub = min(ATTN_SUB, q_ref.shape[1])
    nsub = q_ref.shape[1] // sub

    def scores(j):
        q = q_ref[0, j * sub:(j + 1) * sub, :]
        zero = jnp.zeros_like(q)
        return (_dot_nt(jnp.where(lane < B_DQK, q, zero), k),
                _dot_nt(jnp.where(lane >= B_DQK, q, zero), k))

    def weighted_values(s):
        m = jnp.max(s, axis=-1, keepdims=True)
        e = jnp.exp2((s - m).astype(BF16))
        ox = jnp.dot(e, vx_scr[...], preferred_element_type=F32)
        return ox[:, 0:HEAD], ox[:, HEAD:HEAD + 1]

    s_next = scores(0)
    for j in range(nsub):
        s1, s2 = s_next
        if j + 1 < nsub:
            s_next = scores(j + 1)
        o1, l1 = weighted_values(s1)
        o2, l2 = weighted_values(s2)
        o = o1 * (1.0 / l1) - o2 * (lam / l2)
        o_ref[0, j * sub:(j + 1) * sub, :] = (_rms(o, 1e-5) * sw_ref[...] * post).astype(o_ref.dtype)


def _attn_call(lam2, qkv, sw):
    bn, L, _ = qkv.shape
    tq = min(ATTN_TQ, L)
    return pl.pallas_call(
        _attn_kernel,
        grid=(bn, B_HEADS, L // tq),
        in_specs=[
            pl.BlockSpec(memory_space=pltpu.SMEM),
            pl.BlockSpec((1, tq, HEAD), lambda b, h, i: (b, i, h)),
            pl.BlockSpec((1, L, HEAD), lambda b, h, i: (b, 0, B_HEADS + h)),
            pl.BlockSpec((1, L, HEAD), lambda b, h, i: (b, 0, 2 * B_HEADS + h)),
            pl.BlockSpec((1, HEAD), lambda b, h, i: (0, 0)),
        ],
        out_specs=pl.BlockSpec((1, tq, HEAD), lambda b, h, i: (b, i, h)),
        out_shape=jax.ShapeDtypeStruct((bn, L, B_WIDTH), BF16),
        scratch_shapes=[pltpu.VMEM((L, 2 * HEAD), BF16)],
        name="diff_attn",
        compiler_params=_cparams(("arbitrary", "arbitrary", "arbitrary")),
    )(lam2, qkv, qkv, qkv, sw.reshape(1, HEAD))


def _rglru_kernel(x_ref, g_ref, cw_ref, cb_ref, wg_ref, bg_ref, c8_ref, o_ref,
                  xs, a_f, x_f, a_b, x_b, *, L):
    pad = SUBLANES
    xs[0:pad, :] = jnp.zeros((pad, HEAD), F32)
    xs[pad + L:2 * pad + L, :] = jnp.zeros((pad, HEAD), F32)
    xs[pad:pad + L, :] = x_ref[0]
    cw = cw_ref[...]
    cb = cb_ref[...]
    wg = wg_ref[0]
    bg = bg_ref[0]
    c8 = c8_ref[0]
    tc = min(256, L)
    for ci in range(L // tc):
        r0 = ci * tc
        u = cb
        for j in range(4):
            u = u + xs[pad - 2 + j + r0:pad - 2 + j + r0 + tc, :] * cw[j:j + 1]
        gates = jnp.dot(u.astype(BF16), wg, preferred_element_type=F32)
        for d, (a_scr, x_scr) in enumerate(((a_f, x_f), (a_b, x_b))):
            r = _sigmoid(gates[:, (2 * d) * HEAD:(2 * d + 1) * HEAD] + bg[2 * d:2 * d + 1])
            ig = _sigmoid(gates[:, (2 * d + 1) * HEAD:(2 * d + 2) * HEAD] + bg[2 * d + 1:2 * d + 2])
            log_a = c8[d:d + 1] * r
            a = jnp.exp(log_a)
            a_scr[r0:r0 + tc, :] = a
            x_scr[r0:r0 + tc, :] = jnp.sqrt(1.0 - a * a) * (ig * u)

    rowi = lax.broadcasted_iota(I32, (SUBLANES, HEAD), 0)
    nblk = L // SUBLANES

    def scan_step(i, carry):
        h_fwd, h_bwd = carry
        r0 = pl.multiple_of(i * SUBLANES, SUBLANES)
        a = a_f[pl.ds(r0, SUBLANES), :]
        x = x_f[pl.ds(r0, SUBLANES), :]
        for s in (1, 2, 4):
            ok = rowi >= s
            a_s = jnp.where(ok, pltpu.roll(a, s, 0), 1.0)
            x_s = jnp.where(ok, pltpu.roll(x, s, 0), 0.0)
            x = a * x_s + x
            a = a * a_s
        x_f[pl.ds(r0, SUBLANES), :] = x + a * h_fwd
        h_fwd = (jnp.broadcast_to(x[SUBLANES - 1:SUBLANES], x.shape)
                 + jnp.broadcast_to(a[SUBLANES - 1:SUBLANES], a.shape) * h_fwd)

        r1 = pl.multiple_of((nblk - 1 - i) * SUBLANES, SUBLANES)
        a = a_b[pl.ds(r1, SUBLANES), :]
        x = x_b[pl.ds(r1, SUBLANES), :]
        for s in (1, 2, 4):
            ok = rowi < SUBLANES - s
            a_s = jnp.where(ok, pltpu.roll(a, SUBLANES - s, 0), 1.0)
            x_s = jnp.where(ok, pltpu.roll(x, SUBLANES - s, 0), 0.0)
            x = a * x_s + x
            a = a * a_s
        x_b[pl.ds(r1, SUBLANES), :] = x + a * h_bwd
        h_bwd = jnp.broadcast_to(x[0:1], x.shape) + jnp.broadcast_to(a[0:1], a.shape) * h_bwd
        return h_fwd, h_bwd

    zero_blk = jnp.zeros((SUBLANES, HEAD), F32)
    lax.fori_loop(0, nblk, scan_step, (zero_blk, zero_blk), unroll=8)
    g = g_ref[0]
    gelu = 0.5 * g * (1.0 + jnp.tanh(math.sqrt(2.0 / math.pi) * (g + 0.044715 * (g * g * g))))
    o_ref[0] = (x_f[...] + x_b[...]) * gelu


def _rglru_call(proj, cw, cb, wg, bg, c8):
    bn, L, _ = proj.shape
    nt = C_WIDTH // HEAD
    blk = (1, L, HEAD)
    scr = pltpu.VMEM((L, HEAD), F32)
    return pl.pallas_call(
        functools.partial(_rglru_kernel, L=L),
        grid=(bn, nt),
        in_specs=[
            pl.BlockSpec(blk, lambda b, j: (b, 0, (COL_CX - 3 * B_WIDTH) // HEAD + j)),
            pl.BlockSpec(blk, lambda b, j: (b, 0, (COL_CG - 3 * B_WIDTH) // HEAD + j)),
            pl.BlockSpec((4, HEAD), lambda b, j: (0, j)),
            pl.BlockSpec((1, HEAD), lambda b, j: (0, j)),
            pl.BlockSpec((1, HEAD, 4 * HEAD), lambda b, j: (j, 0, 0)),
            pl.BlockSpec((1, 4, HEAD), lambda b, j: (j, 0, 0)),
            pl.BlockSpec((1, 2, HEAD), lambda b, j: (j, 0, 0)),
        ],
        out_specs=pl.BlockSpec(blk, lambda b, j: (b, 0, j)),
        out_shape=jax.ShapeDtypeStruct((bn, L, C_WIDTH), F32),
        scratch_shapes=[pltpu.VMEM((L + 2 * SUBLANES, HEAD), F32), scr, scr, scr, scr],
        name="rglru",
        compiler_params=_cparams(("arbitrary", "arbitrary")),
    )(proj, proj, cw, cb, wg, bg, c8)


def _out_kernel(oa_ref, ob_ref, yc_ref, x_ref, gm_ref, rgn_ref, w_ref, npost_ref, npre_ref,
                sc_ref, sh_ref, wr_ref, x1_ref, h2_ref, pt_ref):
    c = (_rms(yc_ref[...], 1e-6) * rgn_ref[...]).astype(BF16)
    lhs = jnp.concatenate([oa_ref[...], ob_ref[...], c], axis=1)
    mix = jnp.dot(lhs, w_ref[0], preferred_element_type=F32)
    x1 = x_ref[...] + gm_ref[0] * (_rms(mix, 1e-6) * npost_ref[...])
    x1_ref[...] = x1
    h2 = _rms(x1, 1e-6) * npre_ref[...]
    h2 = h2 * (1.0 + sc_ref[0]) + sh_ref[0]
    hb = h2.astype(BF16)
    lo = lax.bitcast_convert_type(hb[:, :D_MODEL // 2].astype(F32), U32)
    hi = lax.bitcast_convert_type(hb[:, D_MODEL // 2:].astype(F32), U32)
    h2_ref[...] = hi | lax.shift_right_logical(lo, jnp.uint32(16))
    logits = _dot_nt(wr_ref[...], h2, precision=HIGHEST)
    m = jnp.max(logits, axis=0, keepdims=True)
    e = jnp.exp(logits - m)
    pt_ref[...] = e / jnp.sum(e, axis=0, keepdims=True)


def _out_call(oa, ob, yc, x, gm, rgn, w_all, layer, npost, npre, sc, sh, wr_t, L):
    M = x.shape[0]
    tm = 512 if L % 512 == 0 else L
    lt = L // tm
    row = lambda w: pl.BlockSpec((tm, w), lambda i: (i, 0))
    vec = lambda w: pl.BlockSpec((1, w), lambda i: (0, 0))
    per_b = pl.BlockSpec((1, 1, D_MODEL), lambda i: (i // lt, 0, 0))
    return pl.pallas_call(
        _out_kernel,
        grid=(M // tm,),
        in_specs=[
            row(A_WIDTH), row(B_WIDTH), row(C_WIDTH), row(D_MODEL),
            per_b, vec(C_WIDTH),
            pl.BlockSpec((1, D_MODEL, D_MODEL), lambda i: (layer, 0, 0), pipeline_mode=pl.Buffered(1)),
            vec(D_MODEL), vec(D_MODEL), per_b, per_b,
            pl.BlockSpec((N_EXPERTS, D_MODEL), lambda i: (0, 0)),
        ],
        out_specs=[row(D_MODEL), row(D_MODEL // 2), pl.BlockSpec((N_EXPERTS, tm), lambda i: (0, i))],
        out_shape=[
            jax.ShapeDtypeStruct((M, D_MODEL), F32),
            jax.ShapeDtypeStruct((M, D_MODEL // 2), U32),
            jax.ShapeDtypeStruct((N_EXPERTS, M), F32),
        ],
        name="out_proj_router",
        compiler_params=_cparams(("arbitrary",)),
    )(oa, ob, yc, x, gm, rgn.reshape(1, C_WIDTH), w_all, npost.reshape(1, D_MODEL),
      npre.reshape(1, D_MODEL), sc, sh, wr_t)


def _select_kernel(p_ref, sel_ref, *, cap, n):
    bits = lax.bitcast_convert_type(p_ref[...], I32)
    idx = lax.broadcasted_iota(I32, bits.shape, 1)

    def count(mask):
        return jnp.sum(mask.astype(F32), axis=-1, keepdims=True).astype(I32)

    def value_step(i, ans):
        cand = ans | lax.shift_left(jnp.int32(1), 30 - i)
        return jnp.where(count(bits >= cand) >= cap, cand, ans)

    thr = lax.fori_loop(0, 31, value_step, jnp.zeros((N_EXPERTS, 1), I32))
    gt = bits > thr
    eq = bits == thr
    need = cap - count(gt)

    def index_step(i, lohi):
        lo, hi = lohi
        mid = lax.shift_right_arithmetic(lo + hi, 1)
        ok = count(eq & (idx <= mid)) >= need
        return jnp.where(ok, lo, mid + 1), jnp.where(ok, mid, hi)

    steps = max(1, (n - 1).bit_length())
    lo, _ = lax.fori_loop(0, steps, index_step,
                          (jnp.zeros((N_EXPERTS, 1), I32), jnp.full((N_EXPERTS, 1), n - 1, I32)))
    sel_ref[...] = (gt | (eq & (idx <= lo))).astype(I32)


def _select_call(probs_t, cap):
    n = probs_t.shape[1]
    return pl.pallas_call(
        functools.partial(_select_kernel, cap=cap, n=n),
        out_shape=jax.ShapeDtypeStruct((N_EXPERTS, n), I32),
        name="ec_select",
        compiler_params=pltpu.CompilerParams(vmem_limit_bytes=VMEM_LIMIT),
    )(probs_t)


def _compact_kernel(m_ref, p0_ref, p1_ref, excl_ref, c0_ref, c1_ref, *, rows, fill0, fill1):
    mask = m_ref[...]
    mask_f = mask.astype(F32)
    li = lax.broadcasted_iota(I32, (LANES, LANES), 0)
    lj = lax.broadcasted_iota(I32, (LANES, LANES), 1)
    upper = (li <= lj).astype(BF16)
    c_row = jnp.dot(mask.astype(BF16), upper, preferred_element_type=F32)
    rb = min(rows, 256)
    ri = lax.broadcasted_iota(I32, (rb, rb), 0)
    rj = lax.broadcasted_iota(I32, (rb, rb), 1)
    strict = (rj < ri).astype(BF16)
    carry = jnp.zeros((1, LANES), F32)
    offs = []
    for blk in range(rows // rb):
        tot = jnp.broadcast_to(c_row[blk * rb:(blk + 1) * rb, LANES - 1:LANES], (rb, LANES))
        pre = jnp.dot(strict, tot.astype(BF16), preferred_element_type=F32) + carry
        offs.append(pre)
        carry = pre[rb - 1:rb] + tot[rb - 1:rb]
    row_off = offs[0] if len(offs) == 1 else jnp.concatenate(offs, axis=0)
    excl = (row_off + c_row - mask_f).astype(I32)
    excl_ref[...] = excl

    lane = lax.broadcasted_iota(I32, (rows, LANES), 1)
    flat = lax.broadcasted_iota(I32, (rows, LANES), 0) * LANES + lane
    valid = mask
    disp = jnp.where(mask != 0, flat - excl, 0)
    pay0 = p0_ref[...]
    pay1 = p1_ref[...]
    nbits = (rows * LANES - 1).bit_length()
    for bit in range(nbits):
        s = 1 << bit
        if s < LANES:
            def shift(x, s=s):
                t = pltpu.roll(x, LANES - s, 1)
                t2 = pltpu.roll(t, rows - 1, 0)
                return jnp.where(lane < LANES - s, t, t2)
        else:
            def shift(x, s=s):
                return pltpu.roll(x, rows - s // LANES, 0)
        moving = valid & (lax.shift_right_logical(disp, bit) & 1)
        arrive = shift(moving) != 0
        disp = jnp.where(arrive, shift(disp), disp)
        pay0 = jnp.where(arrive, shift(pay0), pay0)
        pay1 = jnp.where(arrive, shift(pay1), pay1)
        valid = jnp.where(arrive, 1, valid & (1 - moving))
    c0_ref[...] = jnp.where(valid != 0, pay0, fill0)
    c1_ref[...] = jnp.where(valid != 0, pay1, fill1)


def _compact_call(mask, pay0, pay1, fill0, fill1):
    rows = mask.shape[0]
    out = jax.ShapeDtypeStruct((rows, LANES), I32)
    return pl.pallas_call(
        functools.partial(_compact_kernel, rows=rows, fill0=fill0, fill1=fill1),
        out_shape=[out, out, out],
        name="ec_compact",
        compiler_params=pltpu.CompilerParams(vmem_limit_bytes=VMEM_LIMIT),
    )(mask, pay0, pay1)


def _ffn_kernel(idx_ref, idx_next_ref, h_hbm, gate_ref, wg_ref, wu_ref, wd_ref, o_ref, buf, sem,
                *, tm, nsteps, nblk):
    step = pl.program_id(0) * nblk + pl.program_id(1)
    slot = lax.rem(step, 2)

    def issue(ref, dst_slot):
        for r in range(tm):
            pltpu.make_async_copy(h_hbm.at[pl.ds(ref[0, 0, r], 1), :], buf.at[dst_slot, pl.ds(r, 1), :],
                                  sem.at[dst_slot]).start()

    def wait(wait_slot):
        pltpu.make_async_copy(h_hbm.at[pl.ds(0, tm), :], buf.at[wait_slot], sem.at[wait_slot]).wait()

    @pl.when(step == 0)
    def _():
        issue(idx_ref, 0)

    wait(slot)
    u = buf[slot]
    x_lo = lax.bitcast_convert_type(lax.shift_left(u, jnp.uint32(16)), F32).astype(BF16)
    x_hi = lax.bitcast_convert_type(u & jnp.uint32(0xFFFF0000), F32).astype(BF16)
    x = jnp.concatenate([x_lo, x_hi], axis=1)
    issue(idx_next_ref, 1 - slot)
    hg = jnp.dot(x, wg_ref[0, 0], preferred_element_type=F32)
    hu = jnp.dot(x, wu_ref[0, 0], preferred_element_type=F32)
    hid = (hg * _sigmoid(hg) * hu).astype(BF16)
    y = jnp.dot(hid, wd_ref[0, 0], preferred_element_type=F32)
    o_ref[...] = y * gate_ref[...]

    @pl.when(step == nsteps - 1)
    def _():
        wait(1 - slot)


def _ffn_call(idx, h2, gates, wg, wu, wd, layer, tm):
    n_e, slots = idx.shape
    nblk = slots // tm
    nsteps = n_e * nblk
    idx3 = idx.reshape(nsteps, 1, tm)

    def nxt(e, j):
        lin = jnp.minimum(e * nblk + j + 1, nsteps - 1)
        return (lin, 0, 0)

    return pl.pallas_call(
        functools.partial(_ffn_kernel, tm=tm, nsteps=nsteps, nblk=nblk),
        grid=(n_e, nblk),
        in_specs=[
            pl.BlockSpec((1, 1, tm), lambda e, j: (e * nblk + j, 0, 0), memory_space=pltpu.SMEM),
            pl.BlockSpec((1, 1, tm), nxt, memory_space=pltpu.SMEM),
            pl.BlockSpec(memory_space=pl.ANY),
            pl.BlockSpec((tm, 1), lambda e, j: (e * nblk + j, 0)),
            pl.BlockSpec((1, 1, D_MODEL, D_EXPERT), lambda e, j: (layer, e, 0, 0), pipeline_mode=pl.Buffered(1)),
            pl.BlockSpec((1, 1, D_MODEL, D_EXPERT), lambda e, j: (layer, e, 0, 0), pipeline_mode=pl.Buffered(1)),
            pl.BlockSpec((1, 1, D_EXPERT, D_MODEL), lambda e, j: (layer, e, 0, 0), pipeline_mode=pl.Buffered(1)),
        ],
        out_specs=pl.BlockSpec((tm, D_MODEL), lambda e, j: (e * nblk + j, 0)),
        out_shape=jax.ShapeDtypeStruct((n_e * slots, D_MODEL), F32),
        scratch_shapes=[pltpu.VMEM((2, tm, D_MODEL // 2), U32), pltpu.SemaphoreType.DMA((2,))],
        name="ec_ffn",
        compiler_params=_cparams(("arbitrary", "arbitrary")),
    )(idx3, idx3, h2, gates, wg, wu, wd)


CMB_VALID, CMB_FIRST, CMB_LAST, CMB_NEWWIN = 1, 2, 4, 8


def _combine_schedule(off, nwin, win):
    ntiles = off.shape[0] - 1
    lo = jnp.minimum(off[:-1] // win, nwin - 1)
    hi = jnp.maximum(lo, jnp.minimum((off[1:] - 1) // win, nwin - 1))
    cnt = hi - lo + 1
    start = jnp.cumsum(cnt) - cnt
    total = start[-1] + cnt[-1]
    k = jnp.arange(ntiles + nwin, dtype=I32)
    valid = k < total
    t = jnp.clip(jnp.sum((start[None, :] <= k[:, None]).astype(I32), axis=1) - 1, 0, ntiles - 1)
    t = jnp.where(valid, t, ntiles - 1)
    w = jnp.where(valid, lo[t] + (k - start[t]), hi[-1])
    first = valid & (k == start[t])
    last = valid & (k == start[t] + cnt[t] - 1)
    neww = valid & (w != jnp.concatenate([jnp.full((1,), -1, I32), w[:-1]]))
    flags = (valid * CMB_VALID + first * CMB_FIRST + last * CMB_LAST + neww * CMB_NEWWIN).astype(I32)
    return t, w.astype(I32), flags


def _combine_kernel(tile_ref, win_ref, flag_ref, src_ref, src_next_ref, tok_ref, ye_hbm, x1_ref,
                    gf_ref, nw_ref, *rest, tt, win, nitems, split_tile):
    if split_tile is None:
        (o_ref,), (zbuf, z_hi, z_lo, acc, sem) = rest[:1], rest[1:]
    else:
        (o_ref, o2_ref), (zbuf, z_hi, z_lo, acc, sem) = rest[:2], rest[2:]
    k = pl.program_id(0)
    flags = flag_ref[k]
    w = win_ref[k]
    slot = lax.rem(w, 2)
    new_window = (flags & CMB_NEWWIN) != 0

    def issue(ref, dst_slot):
        for r in range(win):
            pltpu.make_async_copy(ye_hbm.at[pl.ds(ref[0, 0, r], 1), :], zbuf.at[dst_slot, pl.ds(r, 1), :],
                                  sem.at[dst_slot]).start()

    def wait(wait_slot):
        pltpu.make_async_copy(ye_hbm.at[pl.ds(0, win), :], zbuf.at[wait_slot], sem.at[wait_slot]).wait()

    def accumulate(hi, lo):
        tok_col = tile_ref[k] * tt + lax.broadcasted_iota(I32, (tt, 1), 0)
        seg = jnp.concatenate([(tok_ref[0, kk:kk + 1, :] == tok_col).astype(BF16)
                               for kk in range(win // LANES)], axis=1)
        prev = jnp.where((flags & CMB_FIRST) != 0, 0.0, acc[...])
        acc[...] = (prev + jnp.dot(seg, hi, preferred_element_type=F32)
                    + jnp.dot(seg, lo, preferred_element_type=F32))

    @pl.when(k == 0)
    def _():
        issue(src_ref, slot)

    @pl.when(new_window)
    def _():
        wait(slot)
        issue(src_next_ref, 1 - slot)
        z = zbuf[slot]
        hi = z.astype(BF16)
        lo = (z - hi.astype(F32)).astype(BF16)
        z_hi[...] = hi
        z_lo[...] = lo
        accumulate(hi, lo)

    @pl.when(((flags & CMB_VALID) != 0) & jnp.logical_not(new_window))
    def _():
        accumulate(z_hi[...], z_lo[...])

    @pl.when((flags & CMB_LAST) != 0)
    def _():
        out = x1_ref[...] + gf_ref[0] * (_rms(acc[...], 1e-6) * nw_ref[...])
        if split_tile is None:
            o_ref[...] = out
        else:
            @pl.when(tile_ref[k] < split_tile)
            def _():
                o_ref[...] = out

            @pl.when(tile_ref[k] >= split_tile)
            def _():
                o2_ref[...] = out

    @pl.when(k == nitems - 1)
    def _():
        wait(1 - slot)


def _combine_call(off, src, tok, ye, x1, gf, nw, L, tt, win, split_rows=None):
    M = x1.shape[0]
    lt = L // tt
    nwin = src.shape[0] // win
    tile_k, win_k, flag_k = _combine_schedule(off, nwin, win)
    nitems = tile_k.shape[0]
    nxt = lambda k, t, w, f: (jnp.minimum(w[k] + 1, nwin - 1), 0, 0)
    if split_rows is None:
        split_tile = None
        out_specs = pl.BlockSpec((tt, D_MODEL), lambda k, t, w, f: (t[k], 0))
        out_shape = jax.ShapeDtypeStruct((M, D_MODEL), F32)
    else:
        split_tile = split_rows // tt
        out_specs = [
            pl.BlockSpec((tt, D_MODEL), lambda k, t, w, f: (jnp.minimum(t[k], split_tile - 1), 0)),
            pl.BlockSpec((tt, D_MODEL), lambda k, t, w, f: (jnp.maximum(t[k] - split_tile, 0), 0)),
        ]
        out_shape = [jax.ShapeDtypeStruct((split_rows, D_MODEL), F32),
                     jax.ShapeDtypeStruct((M - split_rows, D_MODEL), F32)]
    grid_spec = pltpu.PrefetchScalarGridSpec(
        num_scalar_prefetch=3,
        grid=(nitems,),
        in_specs=[
            pl.BlockSpec((1, 1, win), lambda k, t, w, f: (w[k], 0, 0), memory_space=pltpu.SMEM),
            pl.BlockSpec((1, 1, win), nxt, memory_space=pltpu.SMEM),
            pl.BlockSpec((1, win // LANES, LANES), lambda k, t, w, f: (w[k], 0, 0)),
            pl.BlockSpec(memory_space=pl.ANY),
            pl.BlockSpec((tt, D_MODEL), lambda k, t, w, f: (t[k], 0)),
            pl.BlockSpec((1, 1, D_MODEL), lambda k, t, w, f: (t[k] // lt, 0, 0)),
            pl.BlockSpec((1, D_MODEL), lambda k, t, w, f: (0, 0)),
        ],
        out_specs=out_specs,
        scratch_shapes=[
            pltpu.VMEM((2, win, D_MODEL), F32),
            pltpu.VMEM((win, D_MODEL), BF16),
            pltpu.VMEM((win, D_MODEL), BF16),
            pltpu.VMEM((tt, D_MODEL), F32),
            pltpu.SemaphoreType.DMA((2,)),
        ],
    )
    src3 = src.reshape(nwin, 1, win)
    return pl.pallas_call(
        functools.partial(_combine_kernel, tt=tt, win=win, nitems=nitems, split_tile=split_tile),
        grid_spec=grid_spec,
        out_shape=out_shape,
        name="ec_combine",
        compiler_params=_cparams(("arbitrary",)),
    )(tile_k, win_k, flag_k, src3, src3, tok.reshape(nwin, win // LANES, LANES), ye, x1, gf,
      nw.reshape(1, D_MODEL))


def _route_group(probs_t, tok_base, slot_base, slots_total, tt):
    n = probs_t.shape[1]
    cap = max(1, EC_FACTOR * n // N_EXPERTS)
    rows = N_EXPERTS * n // LANES
    sel = _select_call(probs_t, cap)
    tok_ids = tok_base + lax.broadcasted_iota(I32, (N_EXPERTS, n), 1)
    excl_e, idx_c, gate_c = _compact_call(
        sel.reshape(rows, LANES), tok_ids.reshape(rows, LANES),
        lax.bitcast_convert_type(probs_t, I32).reshape(rows, LANES), tok_base, 0)
    npair = N_EXPERTS * cap
    idx_e = idx_c.reshape(-1)[:npair].reshape(N_EXPERTS, cap)
    gate_e = lax.bitcast_convert_type(gate_c.reshape(-1)[:npair], F32).reshape(N_EXPERTS, cap)
    e_col = lax.broadcasted_iota(I32, (N_EXPERTS, n), 0)
    src = excl_e.reshape(N_EXPERTS, n) - e_col * cap + e_col * slots_total + slot_base
    excl_t, src_c, tok_c = _compact_call(
        sel.T.reshape(rows, LANES), src.T.reshape(rows, LANES),
        tok_ids.T.reshape(rows, LANES), 0, -1)
    src_t = src_c.reshape(-1)[:npair]
    tok_t = tok_c.reshape(-1)[:npair]
    off = excl_t.reshape(-1)[::tt * N_EXPERTS]
    return idx_e, gate_e, src_t, tok_t, off, npair


def _block_diag_tiles(w):
    nt = C_WIDTH // HEAD
    per = HEAD // C_BLOCK
    w = w.reshape(2, nt, per, C_BLOCK, C_BLOCK)
    eye = jnp.eye(per, dtype=w.dtype)
    t = jnp.einsum('dtpce,pq->dtpcqe', w, eye)
    return t.reshape(2, nt, HEAD, HEAD).transpose(1, 0, 2, 3)


def kernel(x_prompt, x_sample, c_prompt, c_sample, ada_w, ada_b, norm_mix_pre, norm_mix_post, norm_ffn_pre, norm_ffn_post, w_in, hg_lower, hg_norm, dl_q1, dl_k1, dl_q2, dl_k2, dl_subln, conv_w, conv_b, rg_wa, rg_ba, rg_wx, rg_bx, rg_lambda, rg_norm, w_out, w_router, w_gate, w_up, w_down):
    bp, L, D = x_prompt.shape
    bs = x_sample.shape[0]
    bn = bp + bs
    n_p, n_s = bp * L, bs * x_sample.shape[1]
    M = n_p + n_s
    x = jnp.concatenate([x_prompt, x_sample], axis=0)
    c = jnp.concatenate([c_prompt, c_sample], axis=0)
    mod = _ada_call(c, ada_w, ada_b)
    w_in_bf, w_out_bf = _cast_call(w_in), _cast_call(w_out)
    w_gate_bf, w_up_bf, w_down_bf = _cast_call(w_gate), _cast_call(w_up), _cast_call(w_down)

    lb_soft = jax.nn.softmax(hg_lower.astype(F32), axis=0)
    lb_all = jnp.cumsum(lb_soft, axis=0) - lb_soft[0:1]
    half = ROT_DIM // 2
    inv_freq = ROPE_THETA ** (-jnp.arange(half, dtype=F32) / half)
    ang = jnp.arange(L, dtype=F32)[:, None] * inv_freq[None, :]
    cos, sin = jnp.cos(ang), jnp.sin(ang)
    one = jnp.ones((L, B_DQK - ROT_DIM), F32)
    zero = jnp.zeros((L, B_DQK - ROT_DIM), F32)
    zh = jnp.zeros((L, half), F32)
    cos_t = jnp.tile(jnp.concatenate([cos, cos, one], axis=1), (1, 2))
    s1_t = jnp.tile(jnp.concatenate([-sin, zh, zero], axis=1), (1, 2))
    s2_t = jnp.tile(jnp.concatenate([zh, sin, zero], axis=1), (1, 2))

    cap_p = max(1, EC_FACTOR * n_p // N_EXPERTS)
    cap_s = max(1, EC_FACTOR * n_s // N_EXPERTS)
    slots_total = cap_p + cap_s
    tm_ffn = math.gcd(512, math.gcd(cap_p, cap_s))
    tt = 256 if L % 256 == 0 else L
    win = 256

    xf = x
    for l in range(DEPTH):
        m6 = mod[l].reshape(bn, N_MOD, 1, D)
        sh_m, sc_m, g_m, sh_f, sc_f, g_f = (m6[:, i] for i in range(N_MOD))

        proj, qkv = _in_call(xf, norm_mix_pre[l], sc_m, sh_m, w_in_bf, l, cos_t, s1_t, s2_t)

        lb = lb_all[l].reshape(A_HEADS, HEAD)
        par = jnp.stack([jnp.maximum(lb, LB_MIN), 1.0 - lb], axis=1)
        o_a = _hgrn_call(proj, par, hg_norm[l])

        lam_init = 0.8 - 0.6 * math.exp(-0.3 * l)
        lam = (jnp.exp(jnp.sum(dl_q1[l].astype(F32) * dl_k1[l].astype(F32)))
               - jnp.exp(jnp.sum(dl_q2[l].astype(F32) * dl_k2[l].astype(F32))) + lam_init)
        lam2 = jnp.stack([lam, jnp.asarray(1.0 - lam_init, F32)])
        o_b = _attn_call(lam2, qkv, dl_subln[l])

        nt = C_WIDTH // HEAD
        wa_t = _block_diag_tiles(rg_wa[l])
        wx_t = _block_diag_tiles(rg_wx[l])
        wg = jnp.concatenate([wa_t[:, 0], wx_t[:, 0], wa_t[:, 1], wx_t[:, 1]], axis=-1).astype(BF16)
        bg = jnp.stack([rg_ba[l, 0], rg_bx[l, 0], rg_ba[l, 1], rg_bx[l, 1]], axis=0)
        bg = bg.reshape(4, nt, HEAD).transpose(1, 0, 2)
        c8 = (-RG_C * jax.nn.softplus(-rg_lambda[l])).reshape(2, nt, HEAD).transpose(1, 0, 2)
        y_c = _rglru_call(proj, conv_w[l], conv_b[l].reshape(1, C_WIDTH), wg, bg, c8)

        x1, h2, probs_t = _out_call(
            o_a.reshape(M, A_WIDTH), o_b.reshape(M, B_WIDTH), y_c.reshape(M, C_WIDTH),
            xf.reshape(M, D), g_m, rg_norm[l], w_out_bf, l, norm_mix_post[l],
            norm_ffn_pre[l], sc_f, sh_f, w_router[l].T, L)

        ie_p, ge_p, src_p, tok_p, off_p, np_p = _route_group(probs_t[:, :n_p], 0, 0, slots_total, tt)
        ie_s, ge_s, src_s, tok_s, off_s, np_s = _route_group(probs_t[:, n_p:], n_p, cap_p, slots_total, tt)
        idx = jnp.concatenate([ie_p, ie_s], axis=1)
        gates = jnp.concatenate([ge_p, ge_s], axis=1).reshape(N_EXPERTS * slots_total, 1)
        ye = _ffn_call(idx, h2, gates, w_gate_bf, w_up_bf, w_down_bf, l, tm_ffn)

        pad = -(np_p + np_s) % win
        src = jnp.concatenate([src_p, src_s, jnp.zeros((pad,), I32)])
        tok = jnp.concatenate([tok_p, tok_s, jnp.full((pad,), -1, I32)])
        off = jnp.concatenate([off_p, off_s + np_p, jnp.full((1,), np_p + np_s, I32)])
        if l + 1 < DEPTH:
            x2 = _combine_call(off, src, tok, ye, x1, g_f, norm_ffn_post[l], L, tt, win)
            xf = x2.reshape(bn, L, D)
        else:
            y_p, y_s = _combine_call(off, src, tok, ye, x1, g_f, norm_ffn_post[l], L, tt, win,
                                     split_rows=n_p)
    return y_p.reshape(x_prompt.shape), y_s.reshape(x_sample.shape)
```

```python
import functools
import math

import jax
import jax.numpy as jnp
from jax import lax
from jax.experimental import pallas as pl
from jax.experimental.pallas import tpu as pltpu

F32 = jnp.float32
BF16 = jnp.bfloat16
I32 = jnp.int32
U32 = jnp.uint32
HIGHEST = lax.Precision.HIGHEST

D_MODEL = 2048
DEPTH = 2
A_WIDTH = D_MODEL // 4
B_WIDTH = D_MODEL // 2
C_WIDTH = D_MODEL // 4
HEAD = 128
A_HEADS = A_WIDTH // HEAD
B_HEADS = B_WIDTH // HEAD
B_DQK = HEAD // 2
ROT_DIM = B_DQK // 4
ROPE_THETA = 500000.0
LB_MIN = 1e-12
C_BLOCKS = 8
C_BLOCK = C_WIDTH // C_BLOCKS
RG_C = 8.0
N_EXPERTS = 16
EC_FACTOR = 2
D_EXPERT = D_MODEL // 2
N_MOD = 6
IN_COLS = 3 * A_WIDTH + 2 * A_WIDTH + 3 * B_WIDTH + 2 * C_WIDTH
COL_AQ, COL_AFF, COL_AFB, COL_AI, COL_AG = 0, 512, 1024, 1536, 2048
COL_BQ, COL_BK, COL_BV = 2560, 3584, 4608
COL_CX, COL_CG = 5632, 6144

LANES = 128
SUBLANES = 8
VMEM_LIMIT = 56 * 1024 * 1024

HGRN_CHUNK = 64
HGRN_SUB = 16
HGRN_BLOCK = 256
HGRN_SAFE_RANGE = 80.0
ATTN_TQ = 2048
ATTN_SUB = 256


def _cparams(sem):
    return pltpu.CompilerParams(dimension_semantics=sem, vmem_limit_bytes=VMEM_LIMIT)


def _sigmoid(x):
    return 1.0 / (1.0 + jnp.exp(-x))


def _rms(x, eps):
    return x * lax.rsqrt(jnp.mean(x * x, axis=-1, keepdims=True) + eps)


def _dot_nt(a, b, **kw):
    return lax.dot_general(a, b, (((1,), (1,)), ((), ())), preferred_element_type=F32, **kw)


def _dot_tn(a, b):
    return lax.dot_general(a, b, (((0,), (0,)), ((), ())), preferred_element_type=F32)


CAST_BLOCK_BYTES = 8 * 1024 * 1024


def _cast_kernel(x_ref, o_ref):
    o_ref[...] = x_ref[...].astype(o_ref.dtype)


def _cast_call(w):
    cols = w.shape[-1]
    rows = w.size // cols
    rb = max(SUBLANES, min(rows, CAST_BLOCK_BYTES // (4 * cols) // SUBLANES * SUBLANES))
    while rows % rb:
        rb -= SUBLANES
    spec = pl.BlockSpec((rb, cols), lambda i: (i, 0))
    out = pl.pallas_call(
        _cast_kernel,
        grid=(rows // rb,),
        in_specs=[spec],
        out_specs=spec,
        out_shape=jax.ShapeDtypeStruct((rows, cols), BF16),
        name="cast_bf16",
        compiler_params=_cparams(("arbitrary",)),
    )(w.reshape(rows, cols))
    return out.reshape(w.shape)


def _ada_kernel(c_ref, w_ref, b_ref, o_ref):
    c = c_ref[...]
    a = (c * _sigmoid(c)).astype(BF16)
    o_ref[0] = jnp.dot(a, w_ref[0].astype(BF16), preferred_element_type=F32) + b_ref[0]


def _ada_call(c, ada_w, ada_b):
    bn = c.shape[0]
    tn = 1024
    ncol = N_MOD * D_MODEL
    return pl.pallas_call(
        _ada_kernel,
        grid=(DEPTH, ncol // tn),
        in_specs=[
            pl.BlockSpec((bn, D_MODEL), lambda l, j: (0, 0)),
            pl.BlockSpec((1, D_MODEL, tn), lambda l, j: (l, 0, j)),
            pl.BlockSpec((1, 1, tn), lambda l, j: (l, 0, j)),
        ],
        out_specs=pl.BlockSpec((1, bn, tn), lambda l, j: (l, 0, j)),
        out_shape=jax.ShapeDtypeStruct((DEPTH, bn, ncol), F32),
        name="ada_mod",
        compiler_params=_cparams(("arbitrary", "arbitrary")),
    )(c, ada_w, ada_b.reshape(DEPTH, 1, ncol))


IN_TN = 512
IN_B0, IN_B1 = COL_BQ // IN_TN, COL_CX // IN_TN
IN_ROPE = (COL_BV - COL_BQ) // IN_TN
IN_Q = (COL_BK - COL_BQ) // IN_TN
IN_ROWS = 128


def _in_kernel(x_ref, nw_ref, sc_ref, sh_ref, w_ref, cos_ref, s1_ref, s2_ref, ac_ref, b_ref, h_scr):
    j = pl.program_id(1)

    @pl.when(j == 0)
    def _():
        def rows(c, carry):
            r = pl.ds(pl.multiple_of(c * IN_ROWS, IN_ROWS), IN_ROWS)
            h = _rms(x_ref[0, r, :], 1e-6) * nw_ref[...]
            h_scr[r, :] = (h * (1.0 + sc_ref[0]) + sh_ref[0]).astype(BF16)
            return carry
        lax.fori_loop(0, h_scr.shape[0] // IN_ROWS, rows, 0)

    acc = jnp.dot(h_scr[...], w_ref[0], preferred_element_type=F32)
    in_b = (j >= IN_B0) & (j < IN_B1)

    @pl.when(jnp.logical_not(in_b))
    def _():
        ac_ref[0] = acc

    @pl.when(in_b & (j < IN_B0 + IN_ROPE))
    def _():
        mul = jnp.where(j < IN_B0 + IN_Q, B_DQK ** -0.5 * math.log2(math.e), 1.0)
        cos = cos_ref[...]
        s1 = s1_ref[...]
        s2 = s2_ref[...]
        for h in range(IN_TN // HEAD):
            sl = slice(HEAD * h, HEAD * (h + 1))
            x = acc[:, sl]
            xr = x * cos + pltpu.roll(x, HEAD - ROT_DIM // 2, 1) * s1 + pltpu.roll(x, ROT_DIM // 2, 1) * s2
            b_ref[0, :, sl] = (xr * mul).astype(BF16)

    @pl.when(in_b & (j >= IN_B0 + IN_ROPE))
    def _():
        b_ref[0] = acc.astype(BF16)


def _in_call(x, nw, sc, sh, w_all, layer, cos_t, s1_t, s2_t):
    bn, L, _ = x.shape
    tm = min(1024, L)
    tn = IN_TN
    lt = L // tm
    nb = IN_B1 - IN_B0
    tspec = pl.BlockSpec((tm, HEAD), lambda i, j: (i % lt, 0))
    ac_col = lambda j: jnp.where(j < IN_B0, j, jnp.maximum(j - nb, IN_B0 - 1))
    b_col = lambda j: jnp.clip(j - IN_B0, 0, nb - 1)
    return pl.pallas_call(
        _in_kernel,
        grid=(bn * lt, IN_COLS // tn),
        in_specs=[
            pl.BlockSpec((1, tm, D_MODEL), lambda i, j: (i // lt, i % lt, 0)),
            pl.BlockSpec((1, D_MODEL), lambda i, j: (0, 0)),
            pl.BlockSpec((1, 1, D_MODEL), lambda i, j: (i // lt, 0, 0)),
            pl.BlockSpec((1, 1, D_MODEL), lambda i, j: (i // lt, 0, 0)),
            pl.BlockSpec((1, D_MODEL, tn), lambda i, j: (layer, 0, j)),
            tspec, tspec, tspec,
        ],
        out_specs=[
            pl.BlockSpec((1, tm, tn), lambda i, j: (i // lt, i % lt, ac_col(j))),
            pl.BlockSpec((1, tm, tn), lambda i, j: (i // lt, i % lt, b_col(j))),
        ],
        out_shape=[
            jax.ShapeDtypeStruct((bn, L, IN_COLS - 3 * B_WIDTH), F32),
            jax.ShapeDtypeStruct((bn, L, 3 * B_WIDTH), BF16),
        ],
        scratch_shapes=[pltpu.VMEM((tm, D_MODEL), BF16)],
        name="in_proj",
        compiler_params=_cparams(("arbitrary", "arbitrary")),
    )(x, nw.reshape(1, D_MODEL), sc, sh, w_all, cos_t, s1_t, s2_t)


def _sigmoid_pair(z):
    e = jnp.exp(-jnp.abs(z))
    r = 1.0 / (1.0 + e)
    er = e * r
    pos = z >= 0.0
    return jnp.where(pos, r, er), jnp.where(pos, er, r)


def _split3(x):
    hi = x.astype(BF16)
    r = x - hi.astype(F32)
    mid = r.astype(BF16)
    lo = (r - mid.astype(F32)).astype(BF16)
    return hi, mid, lo


def _hgrn_kernel(q_ref, ff_ref, fb_ref, i_ref, g_ref, par_ref, nw_ref, o_ref,
                 qs_scr, b_scr, k_scr, acc_scr, qe_scr, u_scr, dec_scr, st_scr, *, L):
    C, SB = HGRN_CHUNK, HGRN_SUB
    BLK = min(HGRN_BLOCK, L)
    cpb = BLK // C
    nblk = L // BLK
    nchunk = L // C
    nsb = C // SB
    lb_floor = par_ref[0, 0:1, :]
    one_m_lb = par_ref[0, 1:2, :]
    brow = lax.broadcasted_iota(I32, (BLK, BLK), 0)
    bcol = lax.broadcasted_iota(I32, (BLK, BLK), 1)
    same_chunk = (brow // C) == (bcol // C)
    sub_row = lax.broadcasted_iota(I32, (SB, 1), 0)

    ql = q_ref[0]
    qs_scr[...] = ql * _sigmoid_pair(ql)[0]

    def run_dir(f_ref, rev, first):
        causal = same_chunk & ((bcol >= brow) if rev else (bcol <= brow))
        tri = causal.astype(BF16)
        mid_off = C // 2
        end_off = 0 if rev else C - 1
        beg_off = C - 1 if rev else 0

        def gates(bi, rng):
            r0 = pl.multiple_of(bi * BLK, BLK)
            z = f_ref[0, pl.ds(r0, BLK), :]
            sig, sig_neg = _sigmoid_pair(z)
            log_f = jnp.log(lb_floor + one_m_lb * sig)
            k_scr[pl.ds(r0, BLK), :] = one_m_lb * sig_neg
            hi, mid, lo = _split3(log_f)
            b = (jnp.dot(tri, hi, preferred_element_type=F32)
                 + jnp.dot(tri, mid, preferred_element_type=F32)
                 + jnp.dot(tri, lo, preferred_element_type=F32))
            b_scr[pl.ds(r0, BLK), :] = b
            for c in range(cpb):
                m = b[c * C + mid_off:c * C + mid_off + 1]
                rng = jnp.maximum(rng, b[c * C + beg_off:c * C + beg_off + 1] - m)
                rng = jnp.maximum(rng, m - b[c * C + end_off:c * C + end_off + 1])
            return rng

        rng = lax.fori_loop(0, nblk, gates, jnp.zeros((1, HEAD), F32), unroll=4)
        safe = jnp.max(rng) <= HGRN_SAFE_RANGE

        @pl.when(safe)
        def _():
            def intra(bi, carry):
                r0 = pl.multiple_of(bi * BLK, BLK)
                b = b_scr[pl.ds(r0, BLK), :]
                k = k_scr[pl.ds(r0, BLK), :]
                q = qs_scr[pl.ds(r0, BLK), :]
                v_bf = i_ref[0, pl.ds(r0, BLK), :].astype(BF16)
                m = jnp.concatenate(
                    [jnp.broadcast_to(b[c * C + mid_off:c * C + mid_off + 1], (C, HEAD)) for c in range(cpb)], axis=0)
                b_end = jnp.concatenate(
                    [jnp.broadcast_to(b[c * C + end_off:c * C + end_off + 1], (C, HEAD)) for c in range(cpb)], axis=0)
                qt = (q * jnp.exp(b - m)).astype(BF16)
                kt = (k * jnp.exp(m - b)).astype(BF16)
                sc = _dot_nt(qt, kt)
                sc = jnp.where(causal, sc, 0.0).astype(BF16)
                o_intra = jnp.dot(sc, v_bf, preferred_element_type=F32)
                if first:
                    acc_scr[pl.ds(r0, BLK), :] = o_intra
                else:
                    acc_scr[pl.ds(r0, BLK), :] = acc_scr[pl.ds(r0, BLK), :] + o_intra
                qe_scr[pl.ds(r0, BLK), :] = (q * jnp.exp(b)).astype(BF16)
                kend = (k * jnp.exp(b_end - b)).astype(BF16)
                for c in range(cpb):
                    rows = slice(c * C, (c + 1) * C)
                    u_scr[bi * cpb + c] = _dot_tn(v_bf[rows], kend[rows])
                    dec_scr[pl.ds(bi * cpb + c, 1), :] = jnp.exp(b[c * C + end_off:c * C + end_off + 1])
                return carry

            lax.fori_loop(0, nblk, intra, 0, unroll=4)

            def inter(ci, st):
                c = (nchunk - 1 - ci) if rev else ci
                r0 = pl.multiple_of(c * C, C)
                o_state = _dot_nt(qe_scr[pl.ds(r0, C), :], st.astype(BF16))
                acc_scr[pl.ds(r0, C), :] = acc_scr[pl.ds(r0, C), :] + o_state
                return st * dec_scr[pl.ds(c, 1), :] + u_scr[c]

            lax.fori_loop(0, nchunk, inter, jnp.zeros((HEAD, HEAD), F32), unroll=8)

        @pl.when(jnp.logical_not(safe))
        def _():
            order = list(range(nsb - 1, -1, -1)) if rev else list(range(nsb))
            st_scr[...] = jnp.zeros_like(st_scr)

            def body(ci, carry):
                c = (nchunk - 1 - ci) if rev else ci
                r0 = pl.multiple_of(c * C, C)
                b = b_scr[pl.ds(r0, C), :]
                k = k_scr[pl.ds(r0, C), :]
                q = qs_scr[pl.ds(r0, C), :]
                v = i_ref[0, pl.ds(r0, C), :]
                st = st_scr[...]
                o_state = _dot_nt((q * jnp.exp(b)).astype(BF16), st.astype(BF16))
                v_bf = v.astype(BF16)
                for p, blk in enumerate(order):
                    lo = SB * blk
                    b_blk = b[lo:lo + SB]
                    q_blk = q[lo:lo + SB]
                    k_blk = k[lo:lo + SB]
                    v_blk = v[lo:lo + SB]
                    out = o_state[lo:lo + SB]
                    if p > 0:
                        if rev:
                            bound = b[lo + SB:lo + SB + 1]
                            e0, e1 = lo + SB, C
                        else:
                            bound = b[lo - 1:lo]
                            e0, e1 = 0, lo
                        qt = (q_blk * jnp.exp(b_blk - bound)).astype(BF16)
                        kt = (k[e0:e1] * jnp.exp(bound - b[e0:e1])).astype(BF16)
                        sc = _dot_nt(qt, kt)
                        out = out + jnp.dot(sc.astype(BF16), v_bf[e0:e1], preferred_element_type=F32)
                    diag = jnp.zeros((SB, HEAD), F32)
                    for t in range(SB):
                        bt = b_blk[t:t + 1]
                        pm = k_blk * jnp.exp(jnp.minimum(bt - b_blk, 0.0)) * q_blk[t:t + 1]
                        s = jnp.sum(pm, axis=-1, keepdims=True)
                        keep = (sub_row >= t) if rev else (sub_row <= t)
                        s = jnp.where(keep, s, 0.0)
                        o_t = jnp.sum(s * v_blk, axis=0, keepdims=True)
                        diag = jnp.where(sub_row == t, o_t, diag)
                    out = out + diag
                    rows = pl.ds(r0 + lo, SB)
                    if first:
                        acc_scr[rows, :] = out
                    else:
                        acc_scr[rows, :] = acc_scr[rows, :] + out
                b_end = b[0:1] if rev else b[C - 1:C]
                kend = (k * jnp.exp(b_end - b)).astype(BF16)
                st_scr[...] = st * jnp.exp(b_end) + _dot_tn(v_bf, kend)
                return carry

            lax.fori_loop(0, nchunk, body, 0)

    run_dir(ff_ref, False, True)
    run_dir(fb_ref, True, False)
    g = g_ref[0]
    o_ref[0] = (_rms(acc_scr[...], 1e-6) * nw_ref[...] * (g * _sigmoid(g))).astype(o_ref.dtype)


def _hgrn_call(proj, par, nw):
    bn, L, _ = proj.shape
    cb = lambda off: (lambda b, h: (b, 0, off // HEAD + h))
    blk = (1, L, HEAD)
    nchunk = L // HGRN_CHUNK
    seq = pltpu.VMEM((L, HEAD), F32)
    return pl.pallas_call(
        functools.partial(_hgrn_kernel, L=L),
        grid=(bn, A_HEADS),
        in_specs=[
            pl.BlockSpec(blk, cb(COL_AQ)),
            pl.BlockSpec(blk, cb(COL_AFF)),
            pl.BlockSpec(blk, cb(COL_AFB)),
            pl.BlockSpec(blk, cb(COL_AI)),
            pl.BlockSpec(blk, cb(COL_AG)),
            pl.BlockSpec((1, 2, HEAD), lambda b, h: (h, 0, 0)),
            pl.BlockSpec((1, HEAD), lambda b, h: (0, 0)),
        ],
        out_specs=pl.BlockSpec(blk, lambda b, h: (b, 0, h)),
        out_shape=jax.ShapeDtypeStruct((bn, L, A_WIDTH), BF16),
        scratch_shapes=[
            seq, seq, seq, seq,
            pltpu.VMEM((L, HEAD), BF16),
            pltpu.VMEM((nchunk, HEAD, HEAD), F32),
            pltpu.VMEM((nchunk, HEAD), F32),
            pltpu.VMEM((HEAD, HEAD), F32),
        ],
        name="hgrn2",
        compiler_params=_cparams(("arbitrary", "arbitrary")),
    )(proj, proj, proj, proj, proj, par, nw.reshape(1, HEAD))


def _attn_kernel(lam_ref, q_ref, k_ref, v_ref, sw_ref, o_ref, vx_scr):
    @pl.when(pl.program_id(2) == 0)
    def _():
        lane = lax.broadcasted_iota(I32, (vx_scr.shape[0], HEAD), 1)
        vx_scr[:, 0:HEAD] = v_ref[0]
        vx_scr[:, HEAD:2 * HEAD] = jnp.where(lane == 0, 1.0, 0.0).astype(BF16)

    lam = lam_ref[0]
    post = lam_ref[1]
    k = k_ref[0]
    lane = lax.broadcasted_iota(I32, (1, HEAD), 1)
    sub = min(ATTN_SUB, q_ref.shape[1])
    nsub = q_ref.shape[1] // sub

    def scores(j):
        q = q_ref[0, j * sub:(j + 1) * sub, :]
        zero = jnp.zeros_like(q)
        return (_dot_nt(jnp.where(lane < B_DQK, q, zero), k),
                _dot_nt(jnp.where(lane >= B_DQK, q, zero), k))

    def weighted_values(s):
        m = jnp.max(s, axis=-1, keepdims=True)
        e = jnp.exp2((s - m).astype(BF16))
        ox = jnp.dot(e, vx_scr[...], preferred_element_type=F32)
        return ox[:, 0:HEAD], ox[:, HEAD:HEAD + 1]

    s_next = scores(0)
    for j in range(nsub):
        s1, s2 = s_next
        if j + 1 < nsub:
            s_next = scores(j + 1)
        o1, l1 = weighted_values(s1)
        o2, l2 = weighted_values(s2)
        o = o1 * (1.0 / l1) - o2 * (lam / l2)
        o_ref[0, j * sub:(j + 1) * sub, :] = (_rms(o, 1e-5) * sw_ref[...] * post).astype(o_ref.dtype)


def _attn_call(lam2, qkv, sw):
    bn, L, _ = qkv.shape
    tq = min(ATTN_TQ, L)
    return pl.pallas_call(
        _attn_kernel,
        grid=(bn, B_HEADS, L // tq),
        in_specs=[
            pl.BlockSpec(memory_space=pltpu.SMEM),
            pl.BlockSpec((1, tq, HEAD), lambda b, h, i: (b, i, h)),
            pl.BlockSpec((1, L, HEAD), lambda b, h, i: (b, 0, B_HEADS + h)),
            pl.BlockSpec((1, L, HEAD), lambda b, h, i: (b, 0, 2 * B_HEADS + h)),
            pl.BlockSpec((1, HEAD), lambda b, h, i: (0, 0)),
        ],
        out_specs=pl.BlockSpec((1, tq, HEAD), lambda b, h, i: (b, i, h)),
        out_shape=jax.ShapeDtypeStruct((bn, L, B_WIDTH), BF16),
        scratch_shapes=[pltpu.VMEM((L, 2 * HEAD), BF16)],
        name="diff_attn",
        compiler_params=_cparams(("arbitrary", "arbitrary", "arbitrary")),
    )(lam2, qkv, qkv, qkv, sw.reshape(1, HEAD))


def _rglru_kernel(x_ref, g_ref, cw_ref, cb_ref, wg_ref, bg_ref, c8_ref, o_ref,
                  xs, a_f, x_f, a_b, x_b, *, L):
    pad = SUBLANES
    xs[0:pad, :] = jnp.zeros((pad, HEAD), F32)
    xs[pad + L:2 * pad + L, :] = jnp.zeros((pad, HEAD), F32)
    xs[pad:pad + L, :] = x_ref[0]
    cw = cw_ref[...]
    cb = cb_ref[...]
    wg = wg_ref[0]
    bg = bg_ref[0]
    c8 = c8_ref[0]
    tc = min(256, L)
    for ci in range(L // tc):
        r0 = ci * tc
        u = cb
        for j in range(4):
            u = u + xs[pad - 2 + j + r0:pad - 2 + j + r0 + tc, :] * cw[j:j + 1]
        gates = jnp.dot(u.astype(BF16), wg, preferred_element_type=F32)
        for d, (a_scr, x_scr) in enumerate(((a_f, x_f), (a_b, x_b))):
            r = _sigmoid(gates[:, (2 * d) * HEAD:(2 * d + 1) * HEAD] + bg[2 * d:2 * d + 1])
            ig = _sigmoid(gates[:, (2 * d + 1) * HEAD:(2 * d + 2) * HEAD] + bg[2 * d + 1:2 * d + 2])
            log_a = c8[d:d + 1] * r
            a = jnp.exp(log_a)
            a_scr[r0:r0 + tc, :] = a
            x_scr[r0:r0 + tc, :] = jnp.sqrt(1.0 - a * a) * (ig * u)

    rowi = lax.broadcasted_iota(I32, (SUBLANES, HEAD), 0)
    nblk = L // SUBLANES

    def scan_step(i, carry):
        h_fwd, h_bwd = carry
        r0 = pl.multiple_of(i * SUBLANES, SUBLANES)
        a = a_f[pl.ds(r0, SUBLANES), :]
        x = x_f[pl.ds(r0, SUBLANES), :]
        for s in (1, 2, 4):
            ok = rowi >= s
            a_s = jnp.where(ok, pltpu.roll(a, s, 0), 1.0)
            x_s = jnp.where(ok, pltpu.roll(x, s, 0), 0.0)
            x = a * x_s + x
            a = a * a_s
        x_f[pl.ds(r0, SUBLANES), :] = x + a * h_fwd
        h_fwd = (jnp.broadcast_to(x[SUBLANES - 1:SUBLANES], x.shape)
                 + jnp.broadcast_to(a[SUBLANES - 1:SUBLANES], a.shape) * h_fwd)

        r1 = pl.multiple_of((nblk - 1 - i) * SUBLANES, SUBLANES)
        a = a_b[pl.ds(r1, SUBLANES), :]
        x = x_b[pl.ds(r1, SUBLANES), :]
        for s in (1, 2, 4):
            ok = rowi < SUBLANES - s
            a_s = jnp.where(ok, pltpu.roll(a, SUBLANES - s, 0), 1.0)
            x_s = jnp.where(ok, pltpu.roll(x, SUBLANES - s, 0), 0.0)
            x = a * x_s + x
            a = a * a_s
        x_b[pl.ds(r1, SUBLANES), :] = x + a * h_bwd
        h_bwd = jnp.broadcast_to(x[0:1], x.shape) + jnp.broadcast_to(a[0:1], a.shape) * h_bwd
        return h_fwd, h_bwd

    zero_blk = jnp.zeros((SUBLANES, HEAD), F32)
    lax.fori_loop(0, nblk, scan_step, (zero_blk, zero_blk), unroll=8)
    g = g_ref[0]
    gelu = 0.5 * g * (1.0 + jnp.tanh(math.sqrt(2.0 / math.pi) * (g + 0.044715 * (g * g * g))))
    o_ref[0] = (x_f[...] + x_b[...]) * gelu


def _rglru_call(proj, cw, cb, wg, bg, c8):
    bn, L, _ = proj.shape
    nt = C_WIDTH // HEAD
    blk = (1, L, HEAD)
    scr = pltpu.VMEM((L, HEAD), F32)
    return pl.pallas_call(
        functools.partial(_rglru_kernel, L=L),
        grid=(bn, nt),
        in_specs=[
            pl.BlockSpec(blk, lambda b, j: (b, 0, (COL_CX - 3 * B_WIDTH) // HEAD + j)),
            pl.BlockSpec(blk, lambda b, j: (b, 0, (COL_CG - 3 * B_WIDTH) // HEAD + j)),
            pl.BlockSpec((4, HEAD), lambda b, j: (0, j)),
            pl.BlockSpec((1, HEAD), lambda b, j: (0, j)),
            pl.BlockSpec((1, HEAD, 4 * HEAD), lambda b, j: (j, 0, 0)),
            pl.BlockSpec((1, 4, HEAD), lambda b, j: (j, 0, 0)),
            pl.BlockSpec((1, 2, HEAD), lambda b, j: (j, 0, 0)),
        ],
        out_specs=pl.BlockSpec(blk, lambda b, j: (b, 0, j)),
        out_shape=jax.ShapeDtypeStruct((bn, L, C_WIDTH), F32),
        scratch_shapes=[pltpu.VMEM((L + 2 * SUBLANES, HEAD), F32), scr, scr, scr, scr],
        name="rglru",
        compiler_params=_cparams(("arbitrary", "arbitrary")),
    )(proj, proj, cw, cb, wg, bg, c8)


def _out_kernel(oa_ref, ob_ref, yc_ref, x_ref, gm_ref, rgn_ref, w_ref, npost_ref, npre_ref,
                sc_ref, sh_ref, wr_ref, x1_ref, h2_ref, pt_ref):
    c = (_rms(yc_ref[...], 1e-6) * rgn_ref[...]).astype(BF16)
    lhs = jnp.concatenate([oa_ref[...], ob_ref[...], c], axis=1)
    mix = jnp.dot(lhs, w_ref[0], preferred_element_type=F32)
    x1 = x_ref[...] + gm_ref[0] * (_rms(mix, 1e-6) * npost_ref[...])
    x1_ref[...] = x1
    h2 = _rms(x1, 1e-6) * npre_ref[...]
    h2 = h2 * (1.0 + sc_ref[0]) + sh_ref[0]
    hb = h2.astype(BF16)
    lo = lax.bitcast_convert_type(hb[:, :D_MODEL // 2].astype(F32), U32)
    hi = lax.bitcast_convert_type(hb[:, D_MODEL // 2:].astype(F32), U32)
    h2_ref[...] = hi | lax.shift_right_logical(lo, jnp.uint32(16))
    logits = _dot_nt(wr_ref[...], h2, precision=HIGHEST)
    m = jnp.max(logits, axis=0, keepdims=True)
    e = jnp.exp(logits - m)
    pt_ref[...] = e / jnp.sum(e, axis=0, keepdims=True)


def _out_call(oa, ob, yc, x, gm, rgn, w_all, layer, npost, npre, sc, sh, wr_t, L):
    M = x.shape[0]
    tm = 512 if L % 512 == 0 else L
    lt = L // tm
    row = lambda w: pl.BlockSpec((tm, w), lambda i: (i, 0))
    vec = lambda w: pl.BlockSpec((1, w), lambda i: (0, 0))
    per_b = pl.BlockSpec((1, 1, D_MODEL), lambda i: (i // lt, 0, 0))
    return pl.pallas_call(
        _out_kernel,
        grid=(M // tm,),
        in_specs=[
            row(A_WIDTH), row(B_WIDTH), row(C_WIDTH), row(D_MODEL),
            per_b, vec(C_WIDTH),
            pl.BlockSpec((1, D_MODEL, D_MODEL), lambda i: (layer, 0, 0), pipeline_mode=pl.Buffered(1)),
            vec(D_MODEL), vec(D_MODEL), per_b, per_b,
            pl.BlockSpec((N_EXPERTS, D_MODEL), lambda i: (0, 0)),
        ],
        out_specs=[row(D_MODEL), row(D_MODEL // 2), pl.BlockSpec((N_EXPERTS, tm), lambda i: (0, i))],
        out_shape=[
            jax.ShapeDtypeStruct((M, D_MODEL), F32),
            jax.ShapeDtypeStruct((M, D_MODEL // 2), U32),
            jax.ShapeDtypeStruct((N_EXPERTS, M), F32),
        ],
        name="out_proj_router",
        compiler_params=_cparams(("arbitrary",)),
    )(oa, ob, yc, x, gm, rgn.reshape(1, C_WIDTH), w_all, npost.reshape(1, D_MODEL),
      npre.reshape(1, D_MODEL), sc, sh, wr_t)


def _select_kernel(p_ref, sel_ref, *, cap, n):
    bits = lax.bitcast_convert_type(p_ref[...], I32)
    idx = lax.broadcasted_iota(I32, bits.shape, 1)

    def count(mask):
        return jnp.sum(mask.astype(F32), axis=-1, keepdims=True).astype(I32)

    def value_step(i, ans):
        cand = ans | lax.shift_left(jnp.int32(1), 30 - i)
        return jnp.where(count(bits >= cand) >= cap, cand, ans)

    thr = lax.fori_loop(0, 31, value_step, jnp.zeros((N_EXPERTS, 1), I32))
    gt = bits > thr
    eq = bits == thr
    need = cap - count(gt)

    def index_step(i, lohi):
        lo, hi = lohi
        mid = lax.shift_right_arithmetic(lo + hi, 1)
        ok = count(eq & (idx <= mid)) >= need
        return jnp.where(ok, lo, mid + 1), jnp.where(ok, mid, hi)

    steps = max(1, (n - 1).bit_length())
    lo, _ = lax.fori_loop(0, steps, index_step,
                          (jnp.zeros((N_EXPERTS, 1), I32), jnp.full((N_EXPERTS, 1), n - 1, I32)))
    sel_ref[...] = (gt | (eq & (idx <= lo))).astype(I32)


def _select_call(probs_t, cap):
    n = probs_t.shape[1]
    return pl.pallas_call(
        functools.partial(_select_kernel, cap=cap, n=n),
        out_shape=jax.ShapeDtypeStruct((N_EXPERTS, n), I32),
        name="ec_select",
        compiler_params=pltpu.CompilerParams(vmem_limit_bytes=VMEM_LIMIT),
    )(probs_t)


def _compact_kernel(m_ref, p0_ref, p1_ref, excl_ref, c0_ref, c1_ref, *, rows, fill0, fill1):
    mask = m_ref[...]
    mask_f = mask.astype(F32)
    li = lax.broadcasted_iota(I32, (LANES, LANES), 0)
    lj = lax.broadcasted_iota(I32, (LANES, LANES), 1)
    upper = (li <= lj).astype(BF16)
    c_row = jnp.dot(mask.astype(BF16), upper, preferred_element_type=F32)
    rb = min(rows, 256)
    ri = lax.broadcasted_iota(I32, (rb, rb), 0)
    rj = lax.broadcasted_iota(I32, (rb, rb), 1)
    strict = (rj < ri).astype(BF16)
    carry = jnp.zeros((1, LANES), F32)
    offs = []
    for blk in range(rows // rb):
        tot = jnp.broadcast_to(c_row[blk * rb:(blk + 1) * rb, LANES - 1:LANES], (rb, LANES))
        pre = jnp.dot(strict, tot.astype(BF16), preferred_element_type=F32) + carry
        offs.append(pre)
        carry = pre[rb - 1:rb] + tot[rb - 1:rb]
    row_off = offs[0] if len(offs) == 1 else jnp.concatenate(offs, axis=0)
    excl = (row_off + c_row - mask_f).astype(I32)
    excl_ref[...] = excl

    lane = lax.broadcasted_iota(I32, (rows, LANES), 1)
    flat = lax.broadcasted_iota(I32, (rows, LANES), 0) * LANES + lane
    valid = mask
    disp = jnp.where(mask != 0, flat - excl, 0)
    pay0 = p0_ref[...]
    pay1 = p1_ref[...]
    nbits = (rows * LANES - 1).bit_length()
    for bit in range(nbits):
        s = 1 << bit
        if s < LANES:
            def shift(x, s=s):
                t = pltpu.roll(x, LANES - s, 1)
                t2 = pltpu.roll(t, rows - 1, 0)
                return jnp.where(lane < LANES - s, t, t2)
        else:
            def shift(x, s=s):
                return pltpu.roll(x, rows - s // LANES, 0)
        moving = valid & (lax.shift_right_logical(disp, bit) & 1)
        arrive = shift(moving) != 0
        disp = jnp.where(arrive, shift(disp), disp)
        pay0 = jnp.where(arrive, shift(pay0), pay0)
        pay1 = jnp.where(arrive, shift(pay1), pay1)
        valid = jnp.where(arrive, 1, valid & (1 - moving))
    c0_ref[...] = jnp.where(valid != 0, pay0, fill0)
    c1_ref[...] = jnp.where(valid != 0, pay1, fill1)


def _compact_call(mask, pay0, pay1, fill0, fill1):
    rows = mask.shape[0]
    out = jax.ShapeDtypeStruct((rows, LANES), I32)
    return pl.pallas_call(
        functools.partial(_compact_kernel, rows=rows, fill0=fill0, fill1=fill1),
        out_shape=[out, out, out],
        name="ec_compact",
        compiler_params=pltpu.CompilerParams(vmem_limit_bytes=VMEM_LIMIT),
    )(mask, pay0, pay1)


def _ffn_kernel(idx_ref, idx_next_ref, h_hbm, gate_ref, wg_ref, wu_ref, wd_ref, o_ref, buf, sem,
                *, tm, nsteps, nblk):
    step = pl.program_id(0) * nblk + pl.program_id(1)
    slot = lax.rem(step, 2)

    def issue(ref, dst_slot):
        for r in range(tm):
            pltpu.make_async_copy(h_hbm.at[pl.ds(ref[0, 0, r], 1), :], buf.at[dst_slot, pl.ds(r, 1), :],
                                  sem.at[dst_slot]).start()

    def wait(wait_slot):
        pltpu.make_async_copy(h_hbm.at[pl.ds(0, tm), :], buf.at[wait_slot], sem.at[wait_slot]).wait()

    @pl.when(step == 0)
    def _():
        issue(idx_ref, 0)

    wait(slot)
    u = buf[slot]
    x_lo = lax.bitcast_convert_type(lax.shift_left(u, jnp.uint32(16)), F32).astype(BF16)
    x_hi = lax.bitcast_convert_type(u & jnp.uint32(0xFFFF0000), F32).astype(BF16)
    x = jnp.concatenate([x_lo, x_hi], axis=1)
    issue(idx_next_ref, 1 - slot)
    hg = jnp.dot(x, wg_ref[0, 0], preferred_element_type=F32)
    hu = jnp.dot(x, wu_ref[0, 0], preferred_element_type=F32)
    hid = (hg * _sigmoid(hg) * hu).astype(BF16)
    y = jnp.dot(hid, wd_ref[0, 0], preferred_element_type=F32)
    o_ref[...] = y * gate_ref[...]

    @pl.when(step == nsteps - 1)
    def _():
        wait(1 - slot)


def _ffn_call(idx, h2, gates, wg, wu, wd, layer, tm):
    n_e, slots = idx.shape
    nblk = slots // tm
    nsteps = n_e * nblk
    idx3 = idx.reshape(nsteps, 1, tm)

    def nxt(e, j):
        lin = jnp.minimum(e * nblk + j + 1, nsteps - 1)
        return (lin, 0, 0)

    return pl.pallas_call(
        functools.partial(_ffn_kernel, tm=tm, nsteps=nsteps, nblk=nblk),
        grid=(n_e, nblk),
        in_specs=[
            pl.BlockSpec((1, 1, tm), lambda e, j: (e * nblk + j, 0, 0), memory_space=pltpu.SMEM),
            pl.BlockSpec((1, 1, tm), nxt, memory_space=pltpu.SMEM),
            pl.BlockSpec(memory_space=pl.ANY),
            pl.BlockSpec((tm, 1), lambda e, j: (e * nblk + j, 0)),
            pl.BlockSpec((1, 1, D_MODEL, D_EXPERT), lambda e, j: (layer, e, 0, 0), pipeline_mode=pl.Buffered(1)),
            pl.BlockSpec((1, 1, D_MODEL, D_EXPERT), lambda e, j: (layer, e, 0, 0), pipeline_mode=pl.Buffered(1)),
            pl.BlockSpec((1, 1, D_EXPERT, D_MODEL), lambda e, j: (layer, e, 0, 0), pipeline_mode=pl.Buffered(1)),
        ],
        out_specs=pl.BlockSpec((tm, D_MODEL), lambda e, j: (e * nblk + j, 0)),
        out_shape=jax.ShapeDtypeStruct((n_e * slots, D_MODEL), F32),
        scratch_shapes=[pltpu.VMEM((2, tm, D_MODEL // 2), U32), pltpu.SemaphoreType.DMA((2,))],
        name="ec_ffn",
        compiler_params=_cparams(("arbitrary", "arbitrary")),
    )(idx3, idx3, h2, gates, wg, wu, wd)


CMB_VALID, CMB_FIRST, CMB_LAST, CMB_NEWWIN = 1, 2, 4, 8


def _combine_schedule(off, nwin, win):
    ntiles = off.shape[0] - 1
    lo = jnp.minimum(off[:-1] // win, nwin - 1)
    hi = jnp.maximum(lo, jnp.minimum((off[1:] - 1) // win, nwin - 1))
    cnt = hi - lo + 1
    start = jnp.cumsum(cnt) - cnt
    total = start[-1] + cnt[-1]
    k = jnp.arange(ntiles + nwin, dtype=I32)
    valid = k < total
    t = jnp.clip(jnp.sum((start[None, :] <= k[:, None]).astype(I32), axis=1) - 1, 0, ntiles - 1)
    t = jnp.where(valid, t, ntiles - 1)
    w = jnp.where(valid, lo[t] + (k - start[t]), hi[-1])
    first = valid & (k == start[t])
    last = valid & (k == start[t] + cnt[t] - 1)
    neww = valid & (w != jnp.concatenate([jnp.full((1,), -1, I32), w[:-1]]))
    flags = (valid * CMB_VALID + first * CMB_FIRST + last * CMB_LAST + neww * CMB_NEWWIN).astype(I32)
    return t, w.astype(I32), flags


def _combine_kernel(tile_ref, win_ref, flag_ref, src_ref, src_next_ref, tok_ref, ye_hbm, x1_ref,
                    gf_ref, nw_ref, *rest, tt, win, nitems, split_tile):
    if split_tile is None:
        (o_ref,), (zbuf, z_hi, z_lo, acc, sem) = rest[:1], rest[1:]
    else:
        (o_ref, o2_ref), (zbuf, z_hi, z_lo, acc, sem) = rest[:2], rest[2:]
    k = pl.program_id(0)
    flags = flag_ref[k]
    w = win_ref[k]
    slot = lax.rem(w, 2)
    new_window = (flags & CMB_NEWWIN) != 0

    def issue(ref, dst_slot):
        for r in range(win):
            pltpu.make_async_copy(ye_hbm.at[pl.ds(ref[0, 0, r], 1), :], zbuf.at[dst_slot, pl.ds(r, 1), :],
                                  sem.at[dst_slot]).start()

    def wait(wait_slot):
        pltpu.make_async_copy(ye_hbm.at[pl.ds(0, win), :], zbuf.at[wait_slot], sem.at[wait_slot]).wait()

    def accumulate(hi, lo):
        tok_col = tile_ref[k] * tt + lax.broadcasted_iota(I32, (tt, 1), 0)
        seg = jnp.concatenate([(tok_ref[0, kk:kk + 1, :] == tok_col).astype(BF16)
                               for kk in range(win // LANES)], axis=1)
        prev = jnp.where((flags & CMB_FIRST) != 0, 0.0, acc[...])
        acc[...] = (prev + jnp.dot(seg, hi, preferred_element_type=F32)
                    + jnp.dot(seg, lo, preferred_element_type=F32))

    @pl.when(k == 0)
    def _():
        issue(src_ref, slot)

    @pl.when(new_window)
    def _():
        wait(slot)
        issue(src_next_ref, 1 - slot)
        z = zbuf[slot]
        hi = z.astype(BF16)
        lo = (z - hi.astype(F32)).astype(BF16)
        z_hi[...] = hi
        z_lo[...] = lo
        accumulate(hi, lo)

    @pl.when(((flags & CMB_VALID) != 0) & jnp.logical_not(new_window))
    def _():
        accumulate(z_hi[...], z_lo[...])

    @pl.when((flags & CMB_LAST) != 0)
    def _():
        out = x1_ref[...] + gf_ref[0] * (_rms(acc[...], 1e-6) * nw_ref[...])
        if split_tile is None:
            o_ref[...] = out
        else:
            @pl.when(tile_ref[k] < split_tile)
            def _():
                o_ref[...] = out

            @pl.when(tile_ref[k] >= split_tile)
            def _():
                o2_ref[...] = out

    @pl.when(k == nitems - 1)
    def _():
        wait(1 - slot)


def _combine_call(off, src, tok, ye, x1, gf, nw, L, tt, win, split_rows=None):
    M = x1.shape[0]
    lt = L // tt
    nwin = src.shape[0] // win
    tile_k, win_k, flag_k = _combine_schedule(off, nwin, win)
    nitems = tile_k.shape[0]
    nxt = lambda k, t, w, f: (jnp.minimum(w[k] + 1, nwin - 1), 0, 0)
    if split_rows is None:
        split_tile = None
        out_specs = pl.BlockSpec((tt, D_MODEL), lambda k, t, w, f: (t[k], 0))
        out_shape = jax.ShapeDtypeStruct((M, D_MODEL), F32)
    else:
        split_tile = split_rows // tt
        out_specs = [
            pl.BlockSpec((tt, D_MODEL), lambda k, t, w, f: (jnp.minimum(t[k], split_tile - 1), 0)),
            pl.BlockSpec((tt, D_MODEL), lambda k, t, w, f: (jnp.maximum(t[k] - split_tile, 0), 0)),
        ]
        out_shape = [jax.ShapeDtypeStruct((split_rows, D_MODEL), F32),
                     jax.ShapeDtypeStruct((M - split_rows, D_MODEL), F32)]
    grid_spec = pltpu.PrefetchScalarGridSpec(
        num_scalar_prefetch=3,
        grid=(nitems,),
        in_specs=[
            pl.BlockSpec((1, 1, win), lambda k, t, w, f: (w[k], 0, 0), memory_space=pltpu.SMEM),
            pl.BlockSpec((1, 1, win), nxt, memory_space=pltpu.SMEM),
            pl.BlockSpec((1, win // LANES, LANES), lambda k, t, w, f: (w[k], 0, 0)),
            pl.BlockSpec(memory_space=pl.ANY),
            pl.BlockSpec((tt, D_MODEL), lambda k, t, w, f: (t[k], 0)),
            pl.BlockSpec((1, 1, D_MODEL), lambda k, t, w, f: (t[k] // lt, 0, 0)),
            pl.BlockSpec((1, D_MODEL), lambda k, t, w, f: (0, 0)),
        ],
        out_specs=out_specs,
        scratch_shapes=[
            pltpu.VMEM((2, win, D_MODEL), F32),
            pltpu.VMEM((win, D_MODEL), BF16),
            pltpu.VMEM((win, D_MODEL), BF16),
            pltpu.VMEM((tt, D_MODEL), F32),
            pltpu.SemaphoreType.DMA((2,)),
        ],
    )
    src3 = src.reshape(nwin, 1, win)
    return pl.pallas_call(
        functools.partial(_combine_kernel, tt=tt, win=win, nitems=nitems, split_tile=split_tile),
        grid_spec=grid_spec,
        out_shape=out_shape,
        name="ec_combine",
        compiler_params=_cparams(("arbitrary",)),
    )(tile_k, win_k, flag_k, src3, src3, tok.reshape(nwin, win // LANES, LANES), ye, x1, gf,
      nw.reshape(1, D_MODEL))


def _route_group(probs_t, tok_base, slot_base, slots_total, tt):
    n = probs_t.shape[1]
    cap = max(1, EC_FACTOR * n // N_EXPERTS)
    rows = N_EXPERTS * n // LANES
    sel = _select_call(probs_t, cap)
    tok_ids = tok_base + lax.broadcasted_iota(I32, (N_EXPERTS, n), 1)
    excl_e, idx_c, gate_c = _compact_call(
        sel.reshape(rows, LANES), tok_ids.reshape(rows, LANES),
        lax.bitcast_convert_type(probs_t, I32).reshape(rows, LANES), tok_base, 0)
    npair = N_EXPERTS * cap
    idx_e = idx_c.reshape(-1)[:npair].reshape(N_EXPERTS, cap)
    gate_e = lax.bitcast_convert_type(gate_c.reshape(-1)[:npair], F32).reshape(N_EXPERTS, cap)
    e_col = lax.broadcasted_iota(I32, (N_EXPERTS, n), 0)
    src = excl_e.reshape(N_EXPERTS, n) - e_col * cap + e_col * slots_total + slot_base
    excl_t, src_c, tok_c = _compact_call(
        sel.T.reshape(rows, LANES), src.T.reshape(rows, LANES),
        tok_ids.T.reshape(rows, LANES), 0, -1)
    src_t = src_c.reshape(-1)[:npair]
    tok_t = tok_c.reshape(-1)[:npair]
    off = excl_t.reshape(-1)[::tt * N_EXPERTS]
    return idx_e, gate_e, src_t, tok_t, off, npair


def _block_diag_tiles(w):
    nt = C_WIDTH // HEAD
    per = HEAD // C_BLOCK
    w = w.reshape(2, nt, per, C_BLOCK, C_BLOCK)
    eye = jnp.eye(per, dtype=w.dtype)
    t = jnp.einsum('dtpce,pq->dtpcqe', w, eye)
    return t.reshape(2, nt, HEAD, HEAD).transpose(1, 0, 2, 3)


def kernel(x_prompt, x_sample, c_prompt, c_sample, ada_w, ada_b, norm_mix_pre, norm_mix_post, norm_ffn_pre, norm_ffn_post, w_in, hg_lower, hg_norm, dl_q1, dl_k1, dl_q2, dl_k2, dl_subln, conv_w, conv_b, rg_wa, rg_ba, rg_wx, rg_bx, rg_lambda, rg_norm, w_out, w_router, w_gate, w_up, w_down):
    bp, L, D = x_prompt.shape
    bs = x_sample.shape[0]
    bn = bp + bs
    n_p, n_s = bp * L, bs * x_sample.shape[1]
    M = n_p + n_s
    x = jnp.concatenate([x_prompt, x_sample], axis=0)
    c = jnp.concatenate([c_prompt, c_sample], axis=0)
    mod = _ada_call(c, ada_w, ada_b)
    w_in_bf, w_out_bf = _cast_call(w_in), _cast_call(w_out)
    w_gate_bf, w_up_bf, w_down_bf = _cast_call(w_gate), _cast_call(w_up), _cast_call(w_down)

    lb_soft = jax.nn.softmax(hg_lower.astype(F32), axis=0)
    lb_all = jnp.cumsum(lb_soft, axis=0) - lb_soft[0:1]
    half = ROT_DIM // 2
    inv_freq = ROPE_THETA ** (-jnp.arange(half, dtype=F32) / half)
    ang = jnp.arange(L, dtype=F32)[:, None] * inv_freq[None, :]
    cos, sin = jnp.cos(ang), jnp.sin(ang)
    one = jnp.ones((L, B_DQK - ROT_DIM), F32)
    zero = jnp.zeros((L, B_DQK - ROT_DIM), F32)
    zh = jnp.zeros((L, half), F32)
    cos_t = jnp.tile(jnp.concatenate([cos, cos, one], axis=1), (1, 2))
    s1_t = jnp.tile(jnp.concatenate([-sin, zh, zero], axis=1), (1, 2))
    s2_t = jnp.tile(jnp.concatenate([zh, sin, zero], axis=1), (1, 2))

    cap_p = max(1, EC_FACTOR * n_p // N_EXPERTS)
    cap_s = max(1, EC_FACTOR * n_s // N_EXPERTS)
    slots_total = cap_p + cap_s
    tm_ffn = math.gcd(512, math.gcd(cap_p, cap_s))
    tt = 256 if L % 256 == 0 else L
    win = 256

    xf = x
    for l in range(DEPTH):
        m6 = mod[l].reshape(bn, N_MOD, 1, D)
        sh_m, sc_m, g_m, sh_f, sc_f, g_f = (m6[:, i] for i in range(N_MOD))

        proj, qkv = _in_call(xf, norm_mix_pre[l], sc_m, sh_m, w_in_bf, l, cos_t, s1_t, s2_t)

        lb = lb_all[l].reshape(A_HEADS, HEAD)
        par = jnp.stack([jnp.maximum(lb, LB_MIN), 1.0 - lb], axis=1)
        o_a = _hgrn_call(proj, par, hg_norm[l])

        lam_init = 0.8 - 0.6 * math.exp(-0.3 * l)
        lam = (jnp.exp(jnp.sum(dl_q1[l].astype(F32) * dl_k1[l].astype(F32)))
               - jnp.exp(jnp.sum(dl_q2[l].astype(F32) * dl_k2[l].astype(F32))) + lam_init)
        lam2 = jnp.stack([lam, jnp.asarray(1.0 - lam_init, F32)])
        o_b = _attn_call(lam2, qkv, dl_subln[l])

        nt = C_WIDTH // HEAD
        wa_t = _block_diag_tiles(rg_wa[l])
        wx_t = _block_diag_tiles(rg_wx[l])
        wg = jnp.concatenate([wa_t[:, 0], wx_t[:, 0], wa_t[:, 1], wx_t[:, 1]], axis=-1).astype(BF16)
        bg = jnp.stack([rg_ba[l, 0], rg_bx[l, 0], rg_ba[l, 1], rg_bx[l, 1]], axis=0)
        bg = bg.reshape(4, nt, HEAD).transpose(1, 0, 2)
        c8 = (-RG_C * jax.nn.softplus(-rg_lambda[l])).reshape(2, nt, HEAD).transpose(1, 0, 2)
        y_c = _rglru_call(proj, conv_w[l], conv_b[l].reshape(1, C_WIDTH), wg, bg, c8)

        x1, h2, probs_t = _out_call(
            o_a.reshape(M, A_WIDTH), o_b.reshape(M, B_WIDTH), y_c.reshape(M, C_WIDTH),
            xf.reshape(M, D), g_m, rg_norm[l], w_out_bf, l, norm_mix_post[l],
            norm_ffn_pre[l], sc_f, sh_f, w_router[l].T, L)

        ie_p, ge_p, src_p, tok_p, off_p, np_p = _route_group(probs_t[:, :n_p], 0, 0, slots_total, tt)
        ie_s, ge_s, src_s, tok_s, off_s, np_s = _route_group(probs_t[:, n_p:], n_p, cap_p, slots_total, tt)
        idx = jnp.concatenate([ie_p, ie_s], axis=1)
        gates = jnp.concatenate([ge_p, ge_s], axis=1).reshape(N_EXPERTS * slots_total, 1)
        ye = _ffn_call(idx, h2, gates, w_gate_bf, w_up_bf, w_down_bf, l, tm_ffn)

        pad = -(np_p + np_s) % win
        src = jnp.concatenate([src_p, src_s, jnp.zeros((pad,), I32)])
        tok = jnp.concatenate([tok_p, tok_s, jnp.full((pad,), -1, I32)])
        off = jnp.concatenate([off_p, off_s + np_p, jnp.full((1,), np_p + np_s, I32)])
        if l + 1 < DEPTH:
            x2 = _combine_call(off, src, tok, ye, x1, g_f, norm_ffn_post[l], L, tt, win)
            xf = x2.reshape(bn, L, D)
        else:
            y_p, y_s = _combine_call(off, src, tok, ye, x1, g_f, norm_ffn_post[l], L, tt, win,
                                     split_rows=n_p)
    return y_p.reshape(x_prompt.shape), y_s.reshape(x_sample.shape)
```

```python
import functools
import math

import jax
import jax.numpy as jnp
from jax import lax
from jax.experimental import pallas as pl
from jax.experimental.pallas import tpu as pltpu

F32 = jnp.float32
BF16 = jnp.bfloat16
I32 = jnp.int32
U32 = jnp.uint32
HIGHEST = lax.Precision.HIGHEST

D_MODEL = 2048
DEPTH = 2
A_WIDTH = D_MODEL // 4
B_WIDTH = D_MODEL // 2
C_WIDTH = D_MODEL // 4
HEAD = 128
A_HEADS = A_WIDTH // HEAD
B_HEADS = B_WIDTH // HEAD
B_DQK = HEAD // 2
ROT_DIM = B_DQK // 4
ROPE_THETA = 500000.0
LB_MIN = 1e-12
C_BLOCKS = 8
C_BLOCK = C_WIDTH // C_BLOCKS
RG_C = 8.0
N_EXPERTS = 16
EC_FACTOR = 2
D_EXPERT = D_MODEL // 2
N_MOD = 6
IN_COLS = 3 * A_WIDTH + 2 * A_WIDTH + 3 * B_WIDTH + 2 * C_WIDTH
COL_AQ, COL_AFF, COL_AFB, COL_AI, COL_AG = 0, 512, 1024, 1536, 2048
COL_BQ, COL_BK, COL_BV = 2560, 3584, 4608
COL_CX, COL_CG = 5632, 6144

LANES = 128
SUBLANES = 8
VMEM_LIMIT = 56 * 1024 * 1024

HGRN_CHUNK = 64
HGRN_SUB = 16
HGRN_BLOCK = 256
HGRN_SAFE_RANGE = 80.0
ATTN_TQ = 2048
ATTN_SUB = 256


def _cparams(sem):
    return pltpu.CompilerParams(dimension_semantics=sem, vmem_limit_bytes=VMEM_LIMIT)


def _sigmoid(x):
    return 1.0 / (1.0 + jnp.exp(-x))


def _rms(x, eps):
    return x * lax.rsqrt(jnp.mean(x * x, axis=-1, keepdims=True) + eps)


def _dot_nt(a, b, **kw):
    return lax.dot_general(a, b, (((1,), (1,)), ((), ())), preferred_element_type=F32, **kw)


def _dot_tn(a, b):
    return lax.dot_general(a, b, (((0,), (0,)), ((), ())), preferred_element_type=F32)


CAST_BLOCK_BYTES = 8 * 1024 * 1024


def _cast_kernel(x_ref, o_ref):
    o_ref[...] = x_ref[...].astype(o_ref.dtype)


def _cast_call(w):
    cols = w.shape[-1]
    rows = w.size // cols
    rb = max(SUBLANES, min(rows, CAST_BLOCK_BYTES // (4 * cols) // SUBLANES * SUBLANES))
    while rows % rb:
        rb -= SUBLANES
    spec = pl.BlockSpec((rb, cols), lambda i: (i, 0))
    out = pl.pallas_call(
        _cast_kernel,
        grid=(rows // rb,),
        in_specs=[spec],
        out_specs=spec,
        out_shape=jax.ShapeDtypeStruct((rows, cols), BF16),
        name="cast_bf16",
        compiler_params=_cparams(("arbitrary",)),
    )(w.reshape(rows, cols))
    return out.reshape(w.shape)


def _ada_kernel(c_ref, w_ref, b_ref, o_ref):
    c = c_ref[...]
    a = (c * _sigmoid(c)).astype(BF16)
    o_ref[0] = jnp.dot(a, w_ref[0].astype(BF16), preferred_element_type=F32) + b_ref[0]


def _ada_call(c, ada_w, ada_b):
    bn = c.shape[0]
    tn = 1024
    ncol = N_MOD * D_MODEL
    return pl.pallas_call(
        _ada_kernel,
        grid=(DEPTH, ncol // tn),
        in_specs=[
            pl.BlockSpec((bn, D_MODEL), lambda l, j: (0, 0)),
            pl.BlockSpec((1, D_MODEL, tn), lambda l, j: (l, 0, j)),
            pl.BlockSpec((1, 1, tn), lambda l, j: (l, 0, j)),
        ],
        out_specs=pl.BlockSpec((1, bn, tn), lambda l, j: (l, 0, j)),
        out_shape=jax.ShapeDtypeStruct((DEPTH, bn, ncol), F32),
        name="ada_mod",
        compiler_params=_cparams(("arbitrary", "arbitrary")),
    )(c, ada_w, ada_b.reshape(DEPTH, 1, ncol))


IN_TN = 512
IN_B0, IN_B1 = COL_BQ // IN_TN, COL_CX // IN_TN
IN_ROPE = (COL_BV - COL_BQ) // IN_TN
IN_Q = (COL_BK - COL_BQ) // IN_TN
IN_ROWS = 128


def _in_kernel(x_ref, nw_ref, sc_ref, sh_ref, w_ref, cos_ref, s1_ref, s2_ref, ac_ref, b_ref, h_scr):
    j = pl.program_id(1)

    @pl.when(j == 0)
    def _():
        def rows(c, carry):
            r = pl.ds(pl.multiple_of(c * IN_ROWS, IN_ROWS), IN_ROWS)
            h = _rms(x_ref[0, r, :], 1e-6) * nw_ref[...]
            h_scr[r, :] = (h * (1.0 + sc_ref[0]) + sh_ref[0]).astype(BF16)
            return carry
        lax.fori_loop(0, h_scr.shape[0] // IN_ROWS, rows, 0)

    acc = jnp.dot(h_scr[...], w_ref[0], preferred_element_type=F32)
    in_b = (j >= IN_B0) & (j < IN_B1)

    @pl.when(jnp.logical_not(in_b))
    def _():
        ac_ref[0] = acc

    @pl.when(in_b & (j < IN_B0 + IN_ROPE))
    def _():
        mul = jnp.where(j < IN_B0 + IN_Q, B_DQK ** -0.5 * math.log2(math.e), 1.0)
        cos = cos_ref[...]
        s1 = s1_ref[...]
        s2 = s2_ref[...]
        for h in range(IN_TN // HEAD):
            sl = slice(HEAD * h, HEAD * (h + 1))
            x = acc[:, sl]
            xr = x * cos + pltpu.roll(x, HEAD - ROT_DIM // 2, 1) * s1 + pltpu.roll(x, ROT_DIM // 2, 1) * s2
            b_ref[0, :, sl] = (xr * mul).astype(BF16)

    @pl.when(in_b & (j >= IN_B0 + IN_ROPE))
    def _():
        b_ref[0] = acc.astype(BF16)


def _in_call(x, nw, sc, sh, w_all, layer, cos_t, s1_t, s2_t):
    bn, L, _ = x.shape
    tm = min(1024, L)
    tn = IN_TN
    lt = L // tm
    nb = IN_B1 - IN_B0
    tspec = pl.BlockSpec((tm, HEAD), lambda i, j: (i % lt, 0))
    ac_col = lambda j: jnp.where(j < IN_B0, j, jnp.maximum(j - nb, IN_B0 - 1))
    b_col = lambda j: jnp.clip(j - IN_B0, 0, nb - 1)
    return pl.pallas_call(
        _in_kernel,
        grid=(bn * lt, IN_COLS // tn),
        in_specs=[
            pl.BlockSpec((1, tm, D_MODEL), lambda i, j: (i // lt, i % lt, 0)),
            pl.BlockSpec((1, D_MODEL), lambda i, j: (0, 0)),
            pl.BlockSpec((1, 1, D_MODEL), lambda i, j: (i // lt, 0, 0)),
            pl.BlockSpec((1, 1, D_MODEL), lambda i, j: (i // lt, 0, 0)),
            pl.BlockSpec((1, D_MODEL, tn), lambda i, j: (layer, 0, j)),
            tspec, tspec, tspec,
        ],
        out_specs=[
            pl.BlockSpec((1, tm, tn), lambda i, j: (i // lt, i % lt, ac_col(j))),
            pl.BlockSpec((1, tm, tn), lambda i, j: (i // lt, i % lt, b_col(j))),
        ],
        out_shape=[
            jax.ShapeDtypeStruct((bn, L, IN_COLS - 3 * B_WIDTH), F32),
            jax.ShapeDtypeStruct((bn, L, 3 * B_WIDTH), BF16),
        ],
        scratch_shapes=[pltpu.VMEM((tm, D_MODEL), BF16)],
        name="in_proj",
        compiler_params=_cparams(("arbitrary", "arbitrary")),
    )(x, nw.reshape(1, D_MODEL), sc, sh, w_all, cos_t, s1_t, s2_t)


def _sigmoid_pair(z):
    e = jnp.exp(-jnp.abs(z))
    r = 1.0 / (1.0 + e)
    er = e * r
    pos = z >= 0.0
    return jnp.where(pos, r, er), jnp.where(pos, er, r)


def _split3(x):
    hi = x.astype(BF16)
    r = x - hi.astype(F32)
    mid = r.astype(BF16)
    lo = (r - mid.astype(F32)).astype(BF16)
    return hi, mid, lo


def _hgrn_kernel(q_ref, ff_ref, fb_ref, i_ref, g_ref, par_ref, nw_ref, o_ref,
                 qs_scr, b_scr, k_scr, acc_scr, qe_scr, u_scr, dec_scr, st_scr, *, L):
    C, SB = HGRN_CHUNK, HGRN_SUB
    BLK = min(HGRN_BLOCK, L)
    cpb = BLK // C
    nblk = L // BLK
    nchunk = L // C
    nsb = C // SB
    lb_floor = par_ref[0, 0:1, :]
    one_m_lb = par_ref[0, 1:2, :]
    brow = lax.broadcasted_iota(I32, (BLK, BLK), 0)
    bcol = lax.broadcasted_iota(I32, (BLK, BLK), 1)
    same_chunk = (brow // C) == (bcol // C)
    sub_row = lax.broadcasted_iota(I32, (SB, 1), 0)

    ql = q_ref[0]
    qs_scr[...] = ql * _sigmoid_pair(ql)[0]

    def run_dir(f_ref, rev, first):
        causal = same_chunk & ((bcol >= brow) if rev else (bcol <= brow))
        tri = causal.astype(BF16)
        mid_off = C // 2
        end_off = 0 if rev else C - 1
        beg_off = C - 1 if rev else 0

        def gates(bi, rng):
            r0 = pl.multiple_of(bi * BLK, BLK)
            z = f_ref[0, pl.ds(r0, BLK), :]
            sig, sig_neg = _sigmoid_pair(z)
            log_f = jnp.log(lb_floor + one_m_lb * sig)
            k_scr[pl.ds(r0, BLK), :] = one_m_lb * sig_neg
            hi, mid, lo = _split3(log_f)
            b = (jnp.dot(tri, hi, preferred_element_type=F32)
                 + jnp.dot(tri, mid, preferred_element_type=F32)
                 + jnp.dot(tri, lo, preferred_element_type=F32))
            b_scr[pl.ds(r0, BLK), :] = b
            for c in range(cpb):
                m = b[c * C + mid_off:c * C + mid_off + 1]
                rng = jnp.maximum(rng, b[c * C + beg_off:c * C + beg_off + 1] - m)
                rng = jnp.maximum(rng, m - b[c * C + end_off:c * C + end_off + 1])
            return rng

        rng = lax.fori_loop(0, nblk, gates, jnp.zeros((1, HEAD), F32), unroll=4)
        safe = jnp.max(rng) <= HGRN_SAFE_RANGE

        @pl.when(safe)
        def _():
            def intra(bi, carry):
                r0 = pl.multiple_of(bi * BLK, BLK)
                b = b_scr[pl.ds(r0, BLK), :]
                k = k_scr[pl.ds(r0, BLK), :]
                q = qs_scr[pl.ds(r0, BLK), :]
                v_bf = i_ref[0, pl.ds(r0, BLK), :].astype(BF16)
                m = jnp.concatenate(
                    [jnp.broadcast_to(b[c * C + mid_off:c * C + mid_off + 1], (C, HEAD)) for c in range(cpb)], axis=0)
                b_end = jnp.concatenate(
                    [jnp.broadcast_to(b[c * C + end_off:c * C + end_off + 1], (C, HEAD)) for c in range(cpb)], axis=0)
                qt = (q * jnp.exp(b - m)).astype(BF16)
                kt = (k * jnp.exp(m - b)).astype(BF16)
                sc = _dot_nt(qt, kt)
                sc = jnp.where(causal, sc, 0.0).astype(BF16)
                o_intra = jnp.dot(sc, v_bf, preferred_element_type=F32)
                if first:
                    acc_scr[pl.ds(r0, BLK), :] = o_intra
                else:
                    acc_scr[pl.ds(r0, BLK), :] = acc_scr[pl.ds(r0, BLK), :] + o_intra
                qe_scr[pl.ds(r0, BLK), :] = (q * jnp.exp(b)).astype(BF16)
                kend = (k * jnp.exp(b_end - b)).astype(BF16)
                for c in range(cpb):
                    rows = slice(c * C, (c + 1) * C)
                    u_scr[bi * cpb + c] = _dot_tn(v_bf[rows], kend[rows])
                    dec_scr[pl.ds(bi * cpb + c, 1), :] = jnp.exp(b[c * C + end_off:c * C + end_off + 1])
                return carry

            lax.fori_loop(0, nblk, intra, 0, unroll=4)

            def inter(ci, st):
                c = (nchunk - 1 - ci) if rev else ci
                r0 = pl.multiple_of(c * C, C)
                o_state = _dot_nt(qe_scr[pl.ds(r0, C), :], st.astype(BF16))
                acc_scr[pl.ds(r0, C), :] = acc_scr[pl.ds(r0, C), :] + o_state
                return st * dec_scr[pl.ds(c, 1), :] + u_scr[c]

            lax.fori_loop(0, nchunk, inter, jnp.zeros((HEAD, HEAD), F32), unroll=8)

        @pl.when(jnp.logical_not(safe))
        def _():
            order = list(range(nsb - 1, -1, -1)) if rev else list(range(nsb))
            st_scr[...] = jnp.zeros_like(st_scr)

            def body(ci, carry):
                c = (nchunk - 1 - ci) if rev else ci
                r0 = pl.multiple_of(c * C, C)
                b = b_scr[pl.ds(r0, C), :]
                k = k_scr[pl.ds(r0, C), :]
                q = qs_scr[pl.ds(r0, C), :]
                v = i_ref[0, pl.ds(r0, C), :]
                st = st_scr[...]
                o_state = _dot_nt((q * jnp.exp(b)).astype(BF16), st.astype(BF16))
                v_bf = v.astype(BF16)
                for p, blk in enumerate(order):
                    lo = SB * blk
                    b_blk = b[lo:lo + SB]
                    q_blk = q[lo:lo + SB]
                    k_blk = k[lo:lo + SB]
                    v_blk = v[lo:lo + SB]
                    out = o_state[lo:lo + SB]
                    if p > 0:
                        if rev:
                            bound = b[lo + SB:lo + SB + 1]
                            e0, e1 = lo + SB, C
                        else:
                            bound = b[lo - 1:lo]
                            e0, e1 = 0, lo
                        qt = (q_blk * jnp.exp(b_blk - bound)).astype(BF16)
                        kt = (k[e0:e1] * jnp.exp(bound - b[e0:e1])).astype(BF16)
                        sc = _dot_nt(qt, kt)
                        out = out + jnp.dot(sc.astype(BF16), v_bf[e0:e1], preferred_element_type=F32)
                    diag = jnp.zeros((SB, HEAD), F32)
                    for t in range(SB):
                        bt = b_blk[t:t + 1]
                        pm = k_blk * jnp.exp(jnp.minimum(bt - b_blk, 0.0)) * q_blk[t:t + 1]
                        s = jnp.sum(pm, axis=-1, keepdims=True)
                        keep = (sub_row >= t) if rev else (sub_row <= t)
                        s = jnp.where(keep, s, 0.0)
                        o_t = jnp.sum(s * v_blk, axis=0, keepdims=True)
                        diag = jnp.where(sub_row == t, o_t, diag)
                    out = out + diag
                    rows = pl.ds(r0 + lo, SB)
                    if first:
                        acc_scr[rows, :] = out
                    else:
                        acc_scr[rows, :] = acc_scr[rows, :] + out
                b_end = b[0:1] if rev else b[C - 1:C]
                kend = (k * jnp.exp(b_end - b)).astype(BF16)
                st_scr[...] = st * jnp.exp(b_end) + _dot_tn(v_bf, kend)
                return carry

            lax.fori_loop(0, nchunk, body, 0)

    run_dir(ff_ref, False, True)
    run_dir(fb_ref, True, False)
    g = g_ref[0]
    o_ref[0] = (_rms(acc_scr[...], 1e-6) * nw_ref[...] * (g * _sigmoid(g))).astype(o_ref.dtype)


def _hgrn_call(proj, par, nw):
    bn, L, _ = proj.shape
    cb = lambda off: (lambda b, h: (b, 0, off // HEAD + h))
    blk = (1, L, HEAD)
    nchunk = L // HGRN_CHUNK
    seq = pltpu.VMEM((L, HEAD), F32)
    return pl.pallas_call(
        functools.partial(_hgrn_kernel, L=L),
        grid=(bn, A_HEADS),
        in_specs=[
            pl.BlockSpec(blk, cb(COL_AQ)),
            pl.BlockSpec(blk, cb(COL_AFF)),
            pl.BlockSpec(blk, cb(COL_AFB)),
            pl.BlockSpec(blk, cb(COL_AI)),
            pl.BlockSpec(blk, cb(COL_AG)),
            pl.BlockSpec((1, 2, HEAD), lambda b, h: (h, 0, 0)),
            pl.BlockSpec((1, HEAD), lambda b, h: (0, 0)),
        ],
        out_specs=pl.BlockSpec(blk, lambda b, h: (b, 0, h)),
        out_shape=jax.ShapeDtypeStruct((bn, L, A_WIDTH), BF16),
        scratch_shapes=[
            seq, seq, seq, seq,
            pltpu.VMEM((L, HEAD), BF16),
            pltpu.VMEM((nchunk, HEAD, HEAD), F32),
            pltpu.VMEM((nchunk, HEAD), F32),
            pltpu.VMEM((HEAD, HEAD), F32),
        ],
        name="hgrn2",
        compiler_params=_cparams(("arbitrary", "arbitrary")),
    )(proj, proj, proj, proj, proj, par, nw.reshape(1, HEAD))


def _attn_kernel(lam_ref, q_ref, k_ref, v_ref, sw_ref, o_ref, vx_scr):
    @pl.when(pl.program_id(2) == 0)
    def _():
        lane = lax.broadcasted_iota(I32, (vx_scr.shape[0], HEAD), 1)
        vx_scr[:, 0:HEAD] = v_ref[0]
        vx_scr[:, HEAD:2 * HEAD] = jnp.where(lane == 0, 1.0, 0.0).astype(BF16)

    lam = lam_ref[0]
    post = lam_ref[1]
    k = k_ref[0]
    lane = lax.broadcasted_iota(I32, (1, HEAD), 1)
    sub = min(ATTN_SUB, q_ref.shape[1])
    nsub = q_ref.shape[1] // sub

    def scores(j):
        q = q_ref[0, j * sub:(j + 1) * sub, :]
        zero = jnp.zeros_like(q)
        return (_dot_nt(jnp.where(lane < B_DQK, q, zero), k),
                _dot_nt(jnp.where(lane >= B_DQK, q, zero), k))

    def weighted_values(s):
        m = jnp.max(s, axis=-1, keepdims=True)
        e = jnp.exp2((s - m).astype(BF16))
        ox = jnp.dot(e, vx_scr[...], preferred_element_type=F32)
        return ox[:, 0:HEAD], ox[:, HEAD:HEAD + 1]

    s_next = scores(0)
    for j in range(nsub):
        s1, s2 = s_next
        if j + 1 < nsub:
            s_next = scores(j + 1)
        o1, l1 = weighted_values(s1)
        o2, l2 = weighted_values(s2)
        o = o1 * (1.0 / l1) - o2 * (lam / l2)
        o_ref[0, j * sub:(j + 1) * sub, :] = (_rms(o, 1e-5) * sw_ref[...] * post).astype(o_ref.dtype)


def _attn_call(lam2, qkv, sw):
    bn, L, _ = qkv.shape
    tq = min(ATTN_TQ, L)
    return pl.pallas_call(
        _attn_kernel,
        grid=(bn, B_HEADS, L // tq),
        in_specs=[
            pl.BlockSpec(memory_space=pltpu.SMEM),
            pl.BlockSpec((1, tq, HEAD), lambda b, h, i: (b, i, h)),
            pl.BlockSpec((1, L, HEAD), lambda b, h, i: (b, 0, B_HEADS + h)),
            pl.BlockSpec((1, L, HEAD), lambda b, h, i: (b, 0, 2 * B_HEADS + h)),
            pl.BlockSpec((1, HEAD), lambda b, h, i: (0, 0)),
        ],
        out_specs=pl.BlockSpec((1, tq, HEAD), lambda b, h, i: (b, i, h)),
        out_shape=jax.ShapeDtypeStruct((bn, L, B_WIDTH), BF16),
        scratch_shapes=[pltpu.VMEM((L, 2 * HEAD), BF16)],
        name="diff_attn",
        compiler_params=_cparams(("arbitrary", "arbitrary", "arbitrary")),
    )(lam2, qkv, qkv, qkv, sw.reshape(1, HEAD))


def _rglru_kernel(x_ref, g_ref, cw_ref, cb_ref, wg_ref, bg_ref, c8_ref, o_ref,
                  xs, a_f, x_f, a_b, x_b, *, L):
    pad = SUBLANES
    xs[0:pad, :] = jnp.zeros((pad, HEAD), F32)
    xs[pad + L:2 * pad + L, :] = jnp.zeros((pad, HEAD), F32)
    xs[pad:pad + L, :] = x_ref[0]
    cw = cw_ref[...]
    cb = cb_ref[...]
    wg = wg_ref[0]
    bg = bg_ref[0]
    c8 = c8_ref[0]
    tc = min(256, L)
    for ci in range(L // tc):
        r0 = ci * tc
        u = cb
        for j in range(4):
            u = u + xs[pad - 2 + j + r0:pad - 2 + j + r0 + tc, :] * cw[j:j + 1]
        gates = jnp.dot(u.astype(BF16), wg, preferred_element_type=F32)
        for d, (a_scr, x_scr) in enumerate(((a_f, x_f), (a_b, x_b))):
            r = _sigmoid(gates[:, (2 * d) * HEAD:(2 * d + 1) * HEAD] + bg[2 * d:2 * d + 1])
            ig = _sigmoid(gates[:, (2 * d + 1) * HEAD:(2 * d + 2) * HEAD] + bg[2 * d + 1:2 * d + 2])
            log_a = c8[d:d + 1] * r
            a = jnp.exp(log_a)
            a_scr[r0:r0 + tc, :] = a
            x_scr[r0:r0 + tc, :] = jnp.sqrt(1.0 - a * a) * (ig * u)

    rowi = lax.broadcasted_iota(I32, (SUBLANES, HEAD), 0)
    nblk = L // SUBLANES

    def scan_step(i, carry):
        h_fwd, h_bwd = carry
        r0 = pl.multiple_of(i * SUBLANES, SUBLANES)
        a = a_f[pl.ds(r0, SUBLANES), :]
        x = x_f[pl.ds(r0, SUBLANES), :]
        for s in (1, 2, 4):
            ok = rowi >= s
            a_s = jnp.where(ok, pltpu.roll(a, s, 0), 1.0)
            x_s = jnp.where(ok, pltpu.roll(x, s, 0), 0.0)
            x = a * x_s + x
            a = a * a_s
        x_f[pl.ds(r0, SUBLANES), :] = x + a * h_fwd
        h_fwd = (jnp.broadcast_to(x[SUBLANES - 1:SUBLANES], x.shape)
                 + jnp.broadcast_to(a[SUBLANES - 1:SUBLANES], a.shape) * h_fwd)

        r1 = pl.multiple_of((nblk - 1 - i) * SUBLANES, SUBLANES)
        a = a_b[pl.ds(r1, SUBLANES), :]
        x = x_b[pl.ds(r1, SUBLANES), :]
        for s in (1, 2, 4):
            ok = rowi < SUBLANES - s
            a_s = jnp.where(ok, pltpu.roll(a, SUBLANES - s, 0), 1.0)
            x_s = jnp.where(ok, pltpu.roll(x, SUBLANES - s, 0), 0.0)
            x = a * x_s + x
            a = a * a_s
        x_b[pl.ds(r1, SUBLANES), :] = x + a * h_bwd
        h_bwd = jnp.broadcast_to(x[0:1], x.shape) + jnp.broadcast_to(a[0:1], a.shape) * h_bwd
        return h_fwd, h_bwd

    zero_blk = jnp.zeros((SUBLANES, HEAD), F32)
    lax.fori_loop(0, nblk, scan_step, (zero_blk, zero_blk), unroll=8)
    g = g_ref[0]
    gelu = 0.5 * g * (1.0 + jnp.tanh(math.sqrt(2.0 / math.pi) * (g + 0.044715 * (g * g * g))))
    o_ref[0] = (x_f[...] + x_b[...]) * gelu


def _rglru_call(proj, cw, cb, wg, bg, c8):
    bn, L, _ = proj.shape
    nt = C_WIDTH // HEAD
    blk = (1, L, HEAD)
    scr = pltpu.VMEM((L, HEAD), F32)
    return pl.pallas_call(
        functools.partial(_rglru_kernel, L=L),
        grid=(bn, nt),
        in_specs=[
            pl.BlockSpec(blk, lambda b, j: (b, 0, (COL_CX - 3 * B_WIDTH) // HEAD + j)),
            pl.BlockSpec(blk, lambda b, j: (b, 0, (COL_CG - 3 * B_WIDTH) // HEAD + j)),
            pl.BlockSpec((4, HEAD), lambda b, j: (0, j)),
            pl.BlockSpec((1, HEAD), lambda b, j: (0, j)),
            pl.BlockSpec((1, HEAD, 4 * HEAD), lambda b, j: (j, 0, 0)),
            pl.BlockSpec((1, 4, HEAD), lambda b, j: (j, 0, 0)),
            pl.BlockSpec((1, 2, HEAD), lambda b, j: (j, 0, 0)),
        ],
        out_specs=pl.BlockSpec(blk, lambda b, j: (b, 0, j)),
        out_shape=jax.ShapeDtypeStruct((bn, L, C_WIDTH), F32),
        scratch_shapes=[pltpu.VMEM((L + 2 * SUBLANES, HEAD), F32), scr, scr, scr, scr],
        name="rglru",
        compiler_params=_cparams(("arbitrary", "arbitrary")),
    )(proj, proj, cw, cb, wg, bg, c8)


def _out_kernel(oa_ref, ob_ref, yc_ref, x_ref, gm_ref, rgn_ref, w_ref, npost_ref, npre_ref,
                sc_ref, sh_ref, wr_ref, x1_ref, h2_ref, pt_ref):
    c = (_rms(yc_ref[...], 1e-6) * rgn_ref[...]).astype(BF16)
    lhs = jnp.concatenate([oa_ref[...], ob_ref[...], c], axis=1)
    mix = jnp.dot(lhs, w_ref[0], preferred_element_type=F32)
    x1 = x_ref[...] + gm_ref[0] * (_rms(mix, 1e-6) * npost_ref[...])
    x1_ref[...] = x1
    h2 = _rms(x1, 1e-6) * npre_ref[...]
    h2 = h2 * (1.0 + sc_ref[0]) + sh_ref[0]
    hb = h2.astype(BF16)
    lo = lax.bitcast_convert_type(hb[:, :D_MODEL // 2].astype(F32), U32)
    hi = lax.bitcast_convert_type(hb[:, D_MODEL // 2:].astype(F32), U32)
    h2_ref[...] = hi | lax.shift_right_logical(lo, jnp.uint32(16))
    logits = _dot_nt(wr_ref[...], h2, precision=HIGHEST)
    m = jnp.max(logits, axis=0, keepdims=True)
    e = jnp.exp(logits - m)
    pt_ref[...] = e / jnp.sum(e, axis=0, keepdims=True)


def _out_call(oa, ob, yc, x, gm, rgn, w_all, layer, npost, npre, sc, sh, wr_t, L):
    M = x.shape[0]
    tm = 512 if L % 512 == 0 else L
    lt = L // tm
    row = lambda w: pl.BlockSpec((tm, w), lambda i: (i, 0))
    vec = lambda w: pl.BlockSpec((1, w), lambda i: (0, 0))
    per_b = pl.BlockSpec((1, 1, D_MODEL), lambda i: (i // lt, 0, 0))
    return pl.pallas_call(
        _out_kernel,
        grid=(M // tm,),
        in_specs=[
            row(A_WIDTH), row(B_WIDTH), row(C_WIDTH), row(D_MODEL),
            per_b, vec(C_WIDTH),
            pl.BlockSpec((1, D_MODEL, D_MODEL), lambda i: (layer, 0, 0), pipeline_mode=pl.Buffered(1)),
            vec(D_MODEL), vec(D_MODEL), per_b, per_b,
            pl.BlockSpec((N_EXPERTS, D_MODEL), lambda i: (0, 0)),
        ],
        out_specs=[row(D_MODEL), row(D_MODEL // 2), pl.BlockSpec((N_EXPERTS, tm), lambda i: (0, i))],
        out_shape=[
            jax.ShapeDtypeStruct((M, D_MODEL), F32),
            jax.ShapeDtypeStruct((M, D_MODEL // 2), U32),
            jax.ShapeDtypeStruct((N_EXPERTS, M), F32),
        ],
        name="out_proj_router",
        compiler_params=_cparams(("arbitrary",)),
    )(oa, ob, yc, x, gm, rgn.reshape(1, C_WIDTH), w_all, npost.reshape(1, D_MODEL),
      npre.reshape(1, D_MODEL), sc, sh, wr_t)


def _select_kernel(p_ref, sel_ref, *, cap, n):
    bits = lax.bitcast_convert_type(p_ref[...], I32)
    idx = lax.broadcasted_iota(I32, bits.shape, 1)

    def count(mask):
        return jnp.sum(mask.astype(F32), axis=-1, keepdims=True).astype(I32)

    def value_step(i, ans):
        cand = ans | lax.shift_left(jnp.int32(1), 30 - i)
        return jnp.where(count(bits >= cand) >= cap, cand, ans)

    thr = lax.fori_loop(0, 31, value_step, jnp.zeros((N_EXPERTS, 1), I32))
    gt = bits > thr
    eq = bits == thr
    need = cap - count(gt)

    def index_step(i, lohi):
        lo, hi = lohi
        mid = lax.shift_right_arithmetic(lo + hi, 1)
        ok = count(eq & (idx <= mid)) >= need
        return jnp.where(ok, lo, mid + 1), jnp.where(ok, mid, hi)

    steps = max(1, (n - 1).bit_length())
    lo, _ = lax.fori_loop(0, steps, index_step,
                          (jnp.zeros((N_EXPERTS, 1), I32), jnp.full((N_EXPERTS, 1), n - 1, I32)))
    sel_ref[...] = (gt | (eq & (idx <= lo))).astype(I32)


def _select_call(probs_t, cap):
    n = probs_t.shape[1]
    return pl.pallas_call(
        functools.partial(_select_kernel, cap=cap, n=n),
        out_shape=jax.ShapeDtypeStruct((N_EXPERTS, n), I32),
        name="ec_select",
        compiler_params=pltpu.CompilerParams(vmem_limit_bytes=VMEM_LIMIT),
    )(probs_t)


def _compact_kernel(m_ref, p0_ref, p1_ref, excl_ref, c0_ref, c1_ref, *, rows, fill0, fill1):
    mask = m_ref[...]
    mask_f = mask.astype(F32)
    li = lax.broadcasted_iota(I32, (LANES, LANES), 0)
    lj = lax.broadcasted_iota(I32, (LANES, LANES), 1)
    upper = (li <= lj).astype(BF16)
    c_row = jnp.dot(mask.astype(BF16), upper, preferred_element_type=F32)
    rb = min(rows, 256)
    ri = lax.broadcasted_iota(I32, (rb, rb), 0)
    rj = lax.broadcasted_iota(I32, (rb, rb), 1)
    strict = (rj < ri).astype(BF16)
    carry = jnp.zeros((1, LANES), F32)
    offs = []
    for blk in range(rows // rb):
        tot = jnp.broadcast_to(c_row[blk * rb:(blk + 1) * rb, LANES - 1:LANES], (rb, LANES))
        pre = jnp.dot(strict, tot.astype(BF16), preferred_element_type=F32) + carry
        offs.append(pre)
        carry = pre[rb - 1:rb] + tot[rb - 1:rb]
    row_off = offs[0] if len(offs) == 1 else jnp.concatenate(offs, axis=0)
    excl = (row_off + c_row - mask_f).astype(I32)
    excl_ref[...] = excl

    lane = lax.broadcasted_iota(I32, (rows, LANES), 1)
    flat = lax.broadcasted_iota(I32, (rows, LANES), 0) * LANES + lane
    valid = mask
    disp = jnp.where(mask != 0, flat - excl, 0)
    pay0 = p0_ref[...]
    pay1 = p1_ref[...]
    nbits = (rows * LANES - 1).bit_length()
    for bit in range(nbits):
        s = 1 << bit
        if s < LANES:
            def shift(x, s=s):
                t = pltpu.roll(x, LANES - s, 1)
                t2 = pltpu.roll(t, rows - 1, 0)
                return jnp.where(lane < LANES - s, t, t2)
        else:
            def shift(x, s=s):
                return pltpu.roll(x, rows - s // LANES, 0)
        moving = valid & (lax.shift_right_logical(disp, bit) & 1)
        arrive = shift(moving) != 0
        disp = jnp.where(arrive, shift(disp), disp)
        pay0 = jnp.where(arrive, shift(pay0), pay0)
        pay1 = jnp.where(arrive, shift(pay1), pay1)
        valid = jnp.where(arrive, 1, valid & (1 - moving))
    c0_ref[...] = jnp.where(valid != 0, pay0, fill0)
    c1_ref[...] = jnp.where(valid != 0, pay1, fill1)


def _compact_call(mask, pay0, pay1, fill0, fill1):
    rows = mask.shape[0]
    out = jax.ShapeDtypeStruct((rows, LANES), I32)
    return pl.pallas_call(
        functools.partial(_compact_kernel, rows=rows, fill0=fill0, fill1=fill1),
        out_shape=[out, out, out],
        name="ec_compact",
        compiler_params=pltpu.CompilerParams(vmem_limit_bytes=VMEM_LIMIT),
    )(mask, pay0, pay1)


def _ffn_kernel(idx_ref, idx_next_ref, h_hbm, gate_ref, wg_ref, wu_ref, wd_ref, o_ref, buf, sem,
                *, tm, nsteps, nblk):
    step = pl.program_id(0) * nblk + pl.program_id(1)
    slot = lax.rem(step, 2)

    def issue(ref, dst_slot):
        for r in range(tm):
            pltpu.make_async_copy(h_hbm.at[pl.ds(ref[0, 0, r], 1), :], buf.at[dst_slot, pl.ds(r, 1), :],
                                  sem.at[dst_slot]).start(priority=r % 2)

    def wait(wait_slot):
        pltpu.make_async_copy(h_hbm.at[pl.ds(0, tm), :], buf.at[wait_slot], sem.at[wait_slot]).wait()

    @pl.when(step == 0)
    def _():
        issue(idx_ref, 0)

    wait(slot)
    u = buf[slot]
    x_lo = lax.bitcast_convert_type(lax.shift_left(u, jnp.uint32(16)), F32).astype(BF16)
    x_hi = lax.bitcast_convert_type(u & jnp.uint32(0xFFFF0000), F32).astype(BF16)
    x = jnp.concatenate([x_lo, x_hi], axis=1)
    issue(idx_next_ref, 1 - slot)
    hg = jnp.dot(x, wg_ref[0, 0], preferred_element_type=F32)
    hu = jnp.dot(x, wu_ref[0, 0], preferred_element_type=F32)
    hid = (hg * _sigmoid(hg) * hu).astype(BF16)
    y = jnp.dot(hid, wd_ref[0, 0], preferred_element_type=F32)
    o_ref[...] = y * gate_ref[...]

    @pl.when(step == nsteps - 1)
    def _():
        wait(1 - slot)


def _ffn_call(idx, h2, gates, wg, wu, wd, layer, tm):
    n_e, slots = idx.shape
    nblk = slots // tm
    nsteps = n_e * nblk
    idx3 = idx.reshape(nsteps, 1, tm)

    def nxt(e, j):
        lin = jnp.minimum(e * nblk + j + 1, nsteps - 1)
        return (lin, 0, 0)

    return pl.pallas_call(
        functools.partial(_ffn_kernel, tm=tm, nsteps=nsteps, nblk=nblk),
        grid=(n_e, nblk),
        in_specs=[
            pl.BlockSpec((1, 1, tm), lambda e, j: (e * nblk + j, 0, 0), memory_space=pltpu.SMEM),
            pl.BlockSpec((1, 1, tm), nxt, memory_space=pltpu.SMEM),
            pl.BlockSpec(memory_space=pl.ANY),
            pl.BlockSpec((tm, 1), lambda e, j: (e * nblk + j, 0)),
            pl.BlockSpec((1, 1, D_MODEL, D_EXPERT), lambda e, j: (layer, e, 0, 0), pipeline_mode=pl.Buffered(1)),
            pl.BlockSpec((1, 1, D_MODEL, D_EXPERT), lambda e, j: (layer, e, 0, 0), pipeline_mode=pl.Buffered(1)),
            pl.BlockSpec((1, 1, D_EXPERT, D_MODEL), lambda e, j: (layer, e, 0, 0), pipeline_mode=pl.Buffered(1)),
        ],
        out_specs=pl.BlockSpec((tm, D_MODEL), lambda e, j: (e * nblk + j, 0)),
        out_shape=jax.ShapeDtypeStruct((n_e * slots, D_MODEL), F32),
        scratch_shapes=[pltpu.VMEM((2, tm, D_MODEL // 2), U32), pltpu.SemaphoreType.DMA((2,))],
        name="ec_ffn",
        compiler_params=_cparams(("arbitrary", "arbitrary")),
    )(idx3, idx3, h2, gates, wg, wu, wd)


CMB_VALID, CMB_FIRST, CMB_LAST, CMB_NEWWIN = 1, 2, 4, 8


def _combine_schedule(off, nwin, win):
    ntiles = off.shape[0] - 1
    lo = jnp.minimum(off[:-1] // win, nwin - 1)
    hi = jnp.maximum(lo, jnp.minimum((off[1:] - 1) // win, nwin - 1))
    cnt = hi - lo + 1
    start = jnp.cumsum(cnt) - cnt
    total = start[-1] + cnt[-1]
    k = jnp.arange(ntiles + nwin, dtype=I32)
    valid = k < total
    t = jnp.clip(jnp.sum((start[None, :] <= k[:, None]).astype(I32), axis=1) - 1, 0, ntiles - 1)
    t = jnp.where(valid, t, ntiles - 1)
    w = jnp.where(valid, lo[t] + (k - start[t]), hi[-1])
    first = valid & (k == start[t])
    last = valid & (k == start[t] + cnt[t] - 1)
    neww = valid & (w != jnp.concatenate([jnp.full((1,), -1, I32), w[:-1]]))
    flags = (valid * CMB_VALID + first * CMB_FIRST + last * CMB_LAST + neww * CMB_NEWWIN).astype(I32)
    return t, w.astype(I32), flags


def _combine_kernel(tile_ref, win_ref, flag_ref, src_ref, src_next_ref, tok_ref, ye_hbm, x1_ref,
                    gf_ref, nw_ref, *rest, tt, win, nitems, split_tile):
    if split_tile is None:
        (o_ref,), (zbuf, z_hi, z_lo, acc, sem) = rest[:1], rest[1:]
    else:
        (o_ref, o2_ref), (zbuf, z_hi, z_lo, acc, sem) = rest[:2], rest[2:]
    k = pl.program_id(0)
    flags = flag_ref[k]
    w = win_ref[k]
    slot = lax.rem(w, 2)
    new_window = (flags & CMB_NEWWIN) != 0

    def issue(ref, dst_slot):
        for r in range(win):
            pltpu.make_async_copy(ye_hbm.at[pl.ds(ref[0, 0, r], 1), :], zbuf.at[dst_slot, pl.ds(r, 1), :],
                                  sem.at[dst_slot]).start(priority=r % 2)

    def wait(wait_slot):
        pltpu.make_async_copy(ye_hbm.at[pl.ds(0, win), :], zbuf.at[wait_slot], sem.at[wait_slot]).wait()

    def accumulate(hi, lo):
        tok_col = tile_ref[k] * tt + lax.broadcasted_iota(I32, (tt, 1), 0)
        seg = jnp.concatenate([(tok_ref[0, kk:kk + 1, :] == tok_col).astype(BF16)
                               for kk in range(win // LANES)], axis=1)
        prev = jnp.where((flags & CMB_FIRST) != 0, 0.0, acc[...])
        acc[...] = (prev + jnp.dot(seg, hi, preferred_element_type=F32)
                    + jnp.dot(seg, lo, preferred_element_type=F32))

    @pl.when(k == 0)
    def _():
        issue(src_ref, slot)

    @pl.when(new_window)
    def _():
        wait(slot)
        issue(src_next_ref, 1 - slot)
        z = zbuf[slot]
        hi = z.astype(BF16)
        lo = (z - hi.astype(F32)).astype(BF16)
        z_hi[...] = hi
        z_lo[...] = lo
        accumulate(hi, lo)

    @pl.when(((flags & CMB_VALID) != 0) & jnp.logical_not(new_window))
    def _():
        accumulate(z_hi[...], z_lo[...])

    @pl.when((flags & CMB_LAST) != 0)
    def _():
        out = x1_ref[...] + gf_ref[0] * (_rms(acc[...], 1e-6) * nw_ref[...])
        if split_tile is None:
            o_ref[...] = out
        else:
            @pl.when(tile_ref[k] < split_tile)
            def _():
                o_ref[...] = out

            @pl.when(tile_ref[k] >= split_tile)
            def _():
                o2_ref[...] = out

    @pl.when(k == nitems - 1)
    def _():
        wait(1 - slot)


def _combine_call(off, src, tok, ye, x1, gf, nw, L, tt, win, split_rows=None):
    M = x1.shape[0]
    lt = L // tt
    nwin = src.shape[0] // win
    tile_k, win_k, flag_k = _combine_schedule(off, nwin, win)
    nitems = tile_k.shape[0]
    nxt = lambda k, t, w, f: (jnp.minimum(w[k] + 1, nwin - 1), 0, 0)
    if split_rows is None:
        split_tile = None
        out_specs = pl.BlockSpec((tt, D_MODEL), lambda k, t, w, f: (t[k], 0))
        out_shape = jax.ShapeDtypeStruct((M, D_MODEL), F32)
    else:
        split_tile = split_rows // tt
        out_specs = [
            pl.BlockSpec((tt, D_MODEL), lambda k, t, w, f: (jnp.minimum(t[k], split_tile - 1), 0)),
            pl.BlockSpec((tt, D_MODEL), lambda k, t, w, f: (jnp.maximum(t[k] - split_tile, 0), 0)),
        ]
        out_shape = [jax.ShapeDtypeStruct((split_rows, D_MODEL), F32),
                     jax.ShapeDtypeStruct((M - split_rows, D_MODEL), F32)]
    grid_spec = pltpu.PrefetchScalarGridSpec(
        num_scalar_prefetch=3,
        grid=(nitems,),
        in_specs=[
            pl.BlockSpec((1, 1, win), lambda k, t, w, f: (w[k], 0, 0), memory_space=pltpu.SMEM),
            pl.BlockSpec((1, 1, win), nxt, memory_space=pltpu.SMEM),
            pl.BlockSpec((1, win // LANES, LANES), lambda k, t, w, f: (w[k], 0, 0)),
            pl.BlockSpec(memory_space=pl.ANY),
            pl.BlockSpec((tt, D_MODEL), lambda k, t, w, f: (t[k], 0)),
            pl.BlockSpec((1, 1, D_MODEL), lambda k, t, w, f: (t[k] // lt, 0, 0)),
            pl.BlockSpec((1, D_MODEL), lambda k, t, w, f: (0, 0)),
        ],
        out_specs=out_specs,
        scratch_shapes=[
            pltpu.VMEM((2, win, D_MODEL), F32),
            pltpu.VMEM((win, D_MODEL), BF16),
            pltpu.VMEM((win, D_MODEL), BF16),
            pltpu.VMEM((tt, D_MODEL), F32),
            pltpu.SemaphoreType.DMA((2,)),
        ],
    )
    src3 = src.reshape(nwin, 1, win)
    return pl.pallas_call(
        functools.partial(_combine_kernel, tt=tt, win=win, nitems=nitems, split_tile=split_tile),
        grid_spec=grid_spec,
        out_shape=out_shape,
        name="ec_combine",
        compiler_params=_cparams(("arbitrary",)),
    )(tile_k, win_k, flag_k, src3, src3, tok.reshape(nwin, win // LANES, LANES), ye, x1, gf,
      nw.reshape(1, D_MODEL))


def _route_group(probs_t, tok_base, slot_base, slots_total, tt):
    n = probs_t.shape[1]
    cap = max(1, EC_FACTOR * n // N_EXPERTS)
    rows = N_EXPERTS * n // LANES
    sel = _select_call(probs_t, cap)
    tok_ids = tok_base + lax.broadcasted_iota(I32, (N_EXPERTS, n), 1)
    excl_e, idx_c, gate_c = _compact_call(
        sel.reshape(rows, LANES), tok_ids.reshape(rows, LANES),
        lax.bitcast_convert_type(probs_t, I32).reshape(rows, LANES), tok_base, 0)
    npair = N_EXPERTS * cap
    idx_e = idx_c.reshape(-1)[:npair].reshape(N_EXPERTS, cap)
    gate_e = lax.bitcast_convert_type(gate_c.reshape(-1)[:npair], F32).reshape(N_EXPERTS, cap)
    e_col = lax.broadcasted_iota(I32, (N_EXPERTS, n), 0)
    src = excl_e.reshape(N_EXPERTS, n) - e_col * cap + e_col * slots_total + slot_base
    excl_t, src_c, tok_c = _compact_call(
        sel.T.reshape(rows, LANES), src.T.reshape(rows, LANES),
        tok_ids.T.reshape(rows, LANES), 0, -1)
    src_t = src_c.reshape(-1)[:npair]
    tok_t = tok_c.reshape(-1)[:npair]
    off = excl_t.reshape(-1)[::tt * N_EXPERTS]
    return idx_e, gate_e, src_t, tok_t, off, npair


def _block_diag_tiles(w):
    nt = C_WIDTH // HEAD
    per = HEAD // C_BLOCK
    w = w.reshape(2, nt, per, C_BLOCK, C_BLOCK)
    eye = jnp.eye(per, dtype=w.dtype)
    t = jnp.einsum('dtpce,pq->dtpcqe', w, eye)
    return t.reshape(2, nt, HEAD, HEAD).transpose(1, 0, 2, 3)


def kernel(x_prompt, x_sample, c_prompt, c_sample, ada_w, ada_b, norm_mix_pre, norm_mix_post, norm_ffn_pre, norm_ffn_post, w_in, hg_lower, hg_norm, dl_q1, dl_k1, dl_q2, dl_k2, dl_subln, conv_w, conv_b, rg_wa, rg_ba, rg_wx, rg_bx, rg_lambda, rg_norm, w_out, w_router, w_gate, w_up, w_down):
    bp, L, D = x_prompt.shape
    bs = x_sample.shape[0]
    bn = bp + bs
    n_p, n_s = bp * L, bs * x_sample.shape[1]
    M = n_p + n_s
    x = jnp.concatenate([x_prompt, x_sample], axis=0)
    c = jnp.concatenate([c_prompt, c_sample], axis=0)
    mod = _ada_call(c, ada_w, ada_b)
    w_in_bf, w_out_bf = _cast_call(w_in), _cast_call(w_out)
    w_gate_bf, w_up_bf, w_down_bf = _cast_call(w_gate), _cast_call(w_up), _cast_call(w_down)

    lb_soft = jax.nn.softmax(hg_lower.astype(F32), axis=0)
    lb_all = jnp.cumsum(lb_soft, axis=0) - lb_soft[0:1]
    half = ROT_DIM // 2
    inv_freq = ROPE_THETA ** (-jnp.arange(half, dtype=F32) / half)
    ang = jnp.arange(L, dtype=F32)[:, None] * inv_freq[None, :]
    cos, sin = jnp.cos(ang), jnp.sin(ang)
    one = jnp.ones((L, B_DQK - ROT_DIM), F32)
    zero = jnp.zeros((L, B_DQK - ROT_DIM), F32)
    zh = jnp.zeros((L, half), F32)
    cos_t = jnp.tile(jnp.concatenate([cos, cos, one], axis=1), (1, 2))
    s1_t = jnp.tile(jnp.concatenate([-sin, zh, zero], axis=1), (1, 2))
    s2_t = jnp.tile(jnp.concatenate([zh, sin, zero], axis=1), (1, 2))

    cap_p = max(1, EC_FACTOR * n_p // N_EXPERTS)
    cap_s = max(1, EC_FACTOR * n_s // N_EXPERTS)
    slots_total = cap_p + cap_s
    tm_ffn = math.gcd(512, math.gcd(cap_p, cap_s))
    tt = 256 if L % 256 == 0 else L
    win = 256

    xf = x
    for l in range(DEPTH):
        m6 = mod[l].reshape(bn, N_MOD, 1, D)
        sh_m, sc_m, g_m, sh_f, sc_f, g_f = (m6[:, i] for i in range(N_MOD))

        proj, qkv = _in_call(xf, norm_mix_pre[l], sc_m, sh_m, w_in_bf, l, cos_t, s1_t, s2_t)

        lb = lb_all[l].reshape(A_HEADS, HEAD)
        par = jnp.stack([jnp.maximum(lb, LB_MIN), 1.0 - lb], axis=1)
        o_a = _hgrn_call(proj, par, hg_norm[l])

        lam_init = 0.8 - 0.6 * math.exp(-0.3 * l)
        lam = (jnp.exp(jnp.sum(dl_q1[l].astype(F32) * dl_k1[l].astype(F32)))
               - jnp.exp(jnp.sum(dl_q2[l].astype(F32) * dl_k2[l].astype(F32))) + lam_init)
        lam2 = jnp.stack([lam, jnp.asarray(1.0 - lam_init, F32)])
        o_b = _attn_call(lam2, qkv, dl_subln[l])

        nt = C_WIDTH // HEAD
        wa_t = _block_diag_tiles(rg_wa[l])
        wx_t = _block_diag_tiles(rg_wx[l])
        wg = jnp.concatenate([wa_t[:, 0], wx_t[:, 0], wa_t[:, 1], wx_t[:, 1]], axis=-1).astype(BF16)
        bg = jnp.stack([rg_ba[l, 0], rg_bx[l, 0], rg_ba[l, 1], rg_bx[l, 1]], axis=0)
        bg = bg.reshape(4, nt, HEAD).transpose(1, 0, 2)
        c8 = (-RG_C * jax.nn.softplus(-rg_lambda[l])).reshape(2, nt, HEAD).transpose(1, 0, 2)
        y_c = _rglru_call(proj, conv_w[l], conv_b[l].reshape(1, C_WIDTH), wg, bg, c8)

        x1, h2, probs_t = _out_call(
            o_a.reshape(M, A_WIDTH), o_b.reshape(M, B_WIDTH), y_c.reshape(M, C_WIDTH),
            xf.reshape(M, D), g_m, rg_norm[l], w_out_bf, l, norm_mix_post[l],
            norm_ffn_pre[l], sc_f, sh_f, w_router[l].T, L)

        ie_p, ge_p, src_p, tok_p, off_p, np_p = _route_group(probs_t[:, :n_p], 0, 0, slots_total, tt)
        ie_s, ge_s, src_s, tok_s, off_s, np_s = _route_group(probs_t[:, n_p:], n_p, cap_p, slots_total, tt)
        idx = jnp.concatenate([ie_p, ie_s], axis=1)
        gates = jnp.concatenate([ge_p, ge_s], axis=1).reshape(N_EXPERTS * slots_total, 1)
        ye = _ffn_call(idx, h2, gates, w_gate_bf, w_up_bf, w_down_bf, l, tm_ffn)

        pad = -(np_p + np_s) % win
        src = jnp.concatenate([src_p, src_s, jnp.zeros((pad,), I32)])
        tok = jnp.concatenate([tok_p, tok_s, jnp.full((pad,), -1, I32)])
        off = jnp.concatenate([off_p, off_s + np_p, jnp.full((1,), np_p + np_s, I32)])
        if l + 1 < DEPTH:
            x2 = _combine_call(off, src, tok, ye, x1, g_f, norm_ffn_post[l], L, tt, win)
            xf = x2.reshape(bn, L, D)
        else:
            y_p, y_s = _combine_call(off, src, tok, ye, x1, g_f, norm_ffn_post[l], L, tt, win,
                                     split_rows=n_p)
    return y_p.reshape(x_prompt.shape), y_s.reshape(x_sample.shape)
```
